```python
import math
import jax, jax.numpy as jnp
from jax import lax
import numpy as np

D_MODEL = 1024
BATCH = 2
SEQ = 8192
DEPTH = 1
DEC_BATCH = 8
DEC_SEQ = 8192
PAST_LEN = 128

MIX_WIDTH = D_MODEL
ATTN_WIDTH = MIX_WIDTH // 2
REC_WIDTH = MIX_WIDTH - ATTN_WIDTH
ATTN_HEADS = 4
ATTN_VDIM = ATTN_WIDTH // ATTN_HEADS
ATTN_QKDIM = ATTN_VDIM // 2
REC_HEADS = 4
REC_KDIM = REC_WIDTH // REC_HEADS
REC_VDIM = REC_WIDTH // REC_HEADS
D_FF = ((8 * D_MODEL // 3 + 127) // 128) * 128
CONV_WIDTH = 3
Q_BLOCK = 128
CHUNK = 64
NORM_EPS = 1e-6
IN_WIDTH = 3 * ATTN_WIDTH + 5 * REC_WIDTH

kernel_name = "hymba_diffattn_hgrn2_convffn_encoder"


def rms_norm(x, w, eps=NORM_EPS):
    x32 = x.astype(jnp.float32)
    y = x32 * lax.rsqrt(jnp.mean(x32 * x32, axis=-1, keepdims=True) + eps)
    return (y * w.astype(jnp.float32)).astype(x.dtype)


def alibi_slopes(n_heads):
    start = 2.0 ** (-8.0 / n_heads)
    return jnp.asarray(np.array([start ** (i + 1) for i in range(n_heads)], dtype=np.float32))


def diff_attention(q, k, v, lam, slopes):
    B, H, _, L, dk = q.shape
    nb = L // Q_BLOCK
    qb = q.reshape(B, H, 2, nb, Q_BLOCK, dk).transpose(3, 0, 1, 2, 4, 5)
    qpos = jnp.arange(L, dtype=jnp.int32).reshape(nb, Q_BLOCK)
    kpos = jnp.arange(L, dtype=jnp.int32)
    scale = dk ** -0.5

    def block(args):
        qblk, pos = args
        s = jnp.einsum('bhjqd,bhjkd->bhjqk', qblk, k) * scale
        dist = jnp.abs(pos[:, None] - kpos[None, :]).astype(jnp.float32)
        s = s - slopes[None, :, None, None, None] * dist[None, None, None]
        p = jax.nn.softmax(s, axis=-1)
        a = p[:, :, 0] - lam * p[:, :, 1]
        return jnp.einsum('bhqk,bhkv->bhqv', a, v)

    o = lax.map(block, (qb, qpos))
    return o.transpose(1, 0, 3, 2, 4).reshape(B, L, H, -1)


def gla_chunk_scan(q, k, v, logf):
    B, H, L, K = q.shape
    V = v.shape[-1]
    n = L // CHUNK

    def to_chunks(t):
        return t.reshape(B, H, n, CHUNK, t.shape[-1]).transpose(2, 0, 1, 3, 4)

    mask = jnp.tril(jnp.ones((CHUNK, CHUNK), dtype=bool))

    def step(S, xs):
        qc, kc, vc, lfc = xs
        G = jnp.cumsum(lfc, axis=-2)
        o_inter = jnp.einsum('bhtk,bhkv->bhtv', qc * jnp.exp(G), S)
        diff = G[:, :, :, None, :] - G[:, :, None, :, :]
        decay = jnp.exp(jnp.where(mask[:, :, None], diff, -jnp.inf))
        A = jnp.einsum('bhtk,bhsk,bhtsk->bhts', qc, kc, decay)
        o_intra = jnp.einsum('bhts,bhsv->bhtv', A, vc)
        G_last = G[:, :, -1:, :]
        S_new = (jnp.exp(G_last[:, :, 0, :])[..., None] * S
                 + jnp.einsum('bhsk,bhsv->bhkv', kc * jnp.exp(G_last - G), vc))
        return S_new, o_inter + o_intra

    S0 = jnp.zeros((B, H, K, V), jnp.float32)
    _, o = lax.scan(step, S0, (to_chunks(q), to_chunks(k), to_chunks(v), to_chunks(logf)))
    return o.transpose(1, 2, 0, 3, 4).reshape(B, H, L, V)


def encoder_layer(x, l, norm_mix_w, w_in, q_norm_w, k_norm_w, lambda_q1, lambda_k1,
                  lambda_q2, lambda_k2, attn_out_norm_w, lb_fwd, lb_bwd, rec_out_norm_w,
                  w_out, norm_ffn_w, w_up, conv_w, conv_b, w_down):
    f32 = jnp.float32
    B, L, _ = x.shape
    h = rms_norm(x, norm_mix_w)
    proj = h @ w_in
    sizes = [ATTN_WIDTH] * 3 + [REC_WIDTH] * 5
    idx = np.cumsum(sizes)[:-1].tolist()
    aq, ak, av, rq, rf_f, rf_b, ri, rg = jnp.split(proj, idx, axis=-1)

    aq = rms_norm(aq.reshape(B, L, ATTN_HEADS, 2, ATTN_QKDIM), q_norm_w)
    ak = rms_norm(ak.reshape(B, L, ATTN_HEADS, 2, ATTN_QKDIM), k_norm_w)
    aq = aq.transpose(0, 2, 3, 1, 4).astype(f32)
    ak = ak.transpose(0, 2, 3, 1, 4).astype(f32)
    av = av.reshape(B, L, ATTN_HEADS, ATTN_VDIM).transpose(0, 2, 1, 3).astype(f32)
    lam_init = 0.8 - 0.6 * math.exp(-0.3 * l)
    lam = (jnp.exp(jnp.sum(lambda_q1.astype(f32) * lambda_k1.astype(f32)))
           - jnp.exp(jnp.sum(lambda_q2.astype(f32) * lambda_k2.astype(f32))) + lam_init)
    ao = diff_attention(aq, ak, av, lam, alibi_slopes(ATTN_HEADS))
    ao = rms_norm(ao, attn_out_norm_w) * (1.0 - lam_init)
    ao = ao.reshape(B, L, ATTN_WIDTH).astype(x.dtype)

    def heads(t):
        return t.reshape(B, L, REC_HEADS, -1).transpose(0, 2, 1, 3).astype(f32)

    rq_h = heads(jax.nn.silu(rq))
    ri_h = heads(ri)

    def direction(f_logits, lb_table, reverse):
        lb = jnp.cumsum(jax.nn.softmax(lb_table.astype(f32), axis=0), axis=0)[l]
        f = heads(lb + (1.0 - lb) * jax.nn.sigmoid(f_logits.astype(f32)))
        logf = jnp.log(f)
        kk = 1.0 - f
        qq, vv = rq_h, ri_h
        if reverse:
            qq, kk, vv, logf = (jnp.flip(t, axis=2) for t in (qq, kk, vv, logf))
        o = gla_chunk_scan(qq, kk, vv, logf)
        return jnp.flip(o, axis=2) if reverse else o

    ro = direction(rf_f, lb_fwd, False) + direction(rf_b, lb_bwd, True)
    ro = ro.transpose(0, 2, 1, 3)
    ro = rms_norm(ro, rec_out_norm_w) * jax.nn.silu(
        rg.reshape(B, L, REC_HEADS, REC_VDIM).astype(f32))
    ro = ro.reshape(B, L, REC_WIDTH).astype(x.dtype)

    x = x + jnp.concatenate([ao, ro], axis=-1) @ w_out

    h = rms_norm(x, norm_ffn_w)
    u = h @ w_up
    pad = CONV_WIDTH // 2
    u_pad = jnp.pad(u, ((0, 0), (pad, pad), (0, 0)))
    c = conv_b
    for j in range(CONV_WIDTH):
        c = c + u_pad[:, j:j + L] * conv_w[j]
    gate, up = jnp.split(c, 2, axis=-1)
    x = x + (jax.nn.silu(gate) * up) @ w_down
    return x


def setup_inputs(seed: int = 0) -> dict:
    key = jax.random.key(seed)
    ks = jax.random.split(key, 24)
    f32 = jnp.float32

    def nrm(k, shape, scale):
        return jax.random.normal(k, shape, f32) * scale

    return {
        "x_prompt": nrm(ks[0], (BATCH, SEQ, D_MODEL), 1.0),
        "x_sample": nrm(ks[1], (DEC_BATCH, DEC_SEQ, D_MODEL), 1.0),
        "norm_mix_w": 1.0 + nrm(ks[2], (DEPTH, D_MODEL), 0.02),
        "w_in": nrm(ks[3], (DEPTH, D_MODEL, IN_WIDTH), D_MODEL ** -0.5),
        "q_norm_w": 1.0 + nrm(ks[4], (DEPTH, ATTN_QKDIM), 0.02),
        "k_norm_w": 1.0 + nrm(ks[5], (DEPTH, ATTN_QKDIM), 0.02),
        "lambda_q1": nrm(ks[6], (DEPTH, ATTN_QKDIM), 0.1),
        "lambda_k1": nrm(ks[7], (DEPTH, ATTN_QKDIM), 0.1),
        "lambda_q2": nrm(ks[8], (DEPTH, ATTN_QKDIM), 0.1),
        "lambda_k2": nrm(ks[9], (DEPTH, ATTN_QKDIM), 0.1),
        "attn_out_norm_w": 1.0 + nrm(ks[10], (DEPTH, ATTN_VDIM), 0.02),
        "lb_fwd": nrm(ks[11], (DEPTH + 1, REC_WIDTH), 0.5),
        "lb_bwd": nrm(ks[12], (DEPTH + 1, REC_WIDTH), 0.5),
        "rec_out_norm_w": 1.0 + nrm(ks[13], (DEPTH, REC_VDIM), 0.02),
        "w_out": nrm(ks[14], (DEPTH, MIX_WIDTH, D_MODEL), MIX_WIDTH ** -0.5),
        "norm_ffn_w": 1.0 + nrm(ks[15], (DEPTH, D_MODEL), 0.02),
        "w_up": nrm(ks[16], (DEPTH, D_MODEL, 2 * D_FF), D_MODEL ** -0.5),
        "conv_w": nrm(ks[17], (DEPTH, CONV_WIDTH, 2 * D_FF), CONV_WIDTH ** -0.5),
        "conv_b": nrm(ks[18], (DEPTH, 2 * D_FF), 0.02),
        "w_down": nrm(ks[19], (DEPTH, D_FF, D_MODEL), D_FF ** -0.5),
    }


def run_trunk(x, norm_mix_w, w_in, q_norm_w, k_norm_w, lambda_q1, lambda_k1, lambda_q2,
              lambda_k2, attn_out_norm_w, lb_fwd, lb_bwd, rec_out_norm_w, w_out,
              norm_ffn_w, w_up, conv_w, conv_b, w_down):
    for l in range(DEPTH):
        x = encoder_layer(x, l, norm_mix_w[l], w_in[l], q_norm_w[l], k_norm_w[l],
                          lambda_q1[l], lambda_k1[l], lambda_q2[l], lambda_k2[l],
                          attn_out_norm_w[l], lb_fwd, lb_bwd, rec_out_norm_w[l], w_out[l],
                          norm_ffn_w[l], w_up[l], conv_w[l], conv_b[l], w_down[l])
    return x


def reference(x_prompt, x_sample, norm_mix_w, w_in, q_norm_w, k_norm_w, lambda_q1,
              lambda_k1, lambda_q2, lambda_k2, attn_out_norm_w, lb_fwd, lb_bwd,
              rec_out_norm_w, w_out, norm_ffn_w, w_up, conv_w, conv_b, w_down):
    y_prompt = run_trunk(x_prompt, norm_mix_w, w_in, q_norm_w, k_norm_w, lambda_q1,
                         lambda_k1, lambda_q2, lambda_k2, attn_out_norm_w, lb_fwd, lb_bwd,
                         rec_out_norm_w, w_out, norm_ffn_w, w_up, conv_w, conv_b, w_down)
    y_sample = run_trunk(x_sample, norm_mix_w, w_in, q_norm_w, k_norm_w, lambda_q1,
                         lambda_k1, lambda_q2, lambda_k2, attn_out_norm_w, lb_fwd, lb_bwd,
                         rec_out_norm_w, w_out, norm_ffn_w, w_up, conv_w, conv_b, w_down)
    return (y_prompt, y_sample)
```

```python
import functools
import math

import numpy as np
import jax
import jax.numpy as jnp
from jax import lax
from jax.experimental import pallas as pl
from jax.experimental.pallas import tpu as pltpu

F32 = jnp.float32
BF16 = jnp.bfloat16

NORM_EPS = 1e-6
ATTN_HEADS = 4
ATTN_QKDIM = 64
ATTN_VDIM = 128
REC_HEADS = 4
REC_DIM = 128
GROUP_W = 512
LOG2E = 1.4426950408889634
LAM_INIT = 0.8 - 0.6 * math.exp(-0.3 * 0)

V7X_VMEM_LIMIT = 56 * 1024 * 1024

TOKEN_TILE = 512
ATTN_BQ = 256
ATTN_BK = 512
ATTN_VROWS = ATTN_VDIM + 8
GLA_CHUNK = 128
GLA_TILE = 512
FFN_FT = 256
HALO = 16


def _cparams(sem):
    return pltpu.CompilerParams(dimension_semantics=sem, vmem_limit_bytes=V7X_VMEM_LIMIT)


def _const_spec(shape):
    zeros = (0,) * len(shape)
    return pl.BlockSpec(shape, lambda *_: zeros)


def _inproj_body(x_ref, nw_ref, w_ref, seg_ref, qw_ref, kw_ref, lbf_ref, lbb_ref,
                 q_ref, k_ref, v_ref, rq_ref, kf_ref, kb_ref, lf_ref, lb_ref, ri_ref, sg_ref):
    x = x_ref[...]
    ms = jnp.mean(x * x, axis=-1, keepdims=True)
    h = (x * lax.rsqrt(ms + NORM_EPS) * nw_ref[...]).astype(BF16)

    def proj(g):
        return jnp.dot(h, w_ref[:, g * GROUP_W:(g + 1) * GROUP_W], preferred_element_type=F32)

    def head_norm(p, w):
        p2 = p * p
        hi = p2.astype(BF16)
        lo = (p2 - hi.astype(F32)).astype(BF16)
        seg = seg_ref[...]
        m = (jnp.dot(hi, seg, preferred_element_type=F32)
             + jnp.dot(lo, seg, preferred_element_type=F32))
        return p * lax.rsqrt(m + NORM_EPS) * w

    def sigmoid(p):
        return 1.0 / (1.0 + jnp.exp(-p))

    q_ref[...] = (head_norm(proj(0), qw_ref[...]) * (ATTN_QKDIM ** -0.5 * LOG2E)).astype(BF16)
    k_ref[...] = head_norm(proj(1), kw_ref[...]).astype(BF16)
    v_ref[...] = proj(2).astype(BF16)
    p = proj(3)
    rq_ref[...] = (p * sigmoid(p)).astype(BF16)
    for g, lbr, kr, lr in ((4, lbf_ref, kf_ref, lf_ref), (5, lbb_ref, kb_ref, lb_ref)):
        lbv = lbr[...]
        f = lbv + (1.0 - lbv) * sigmoid(proj(g))
        kr[...] = (1.0 - f).astype(BF16)
        lr[...] = jnp.log(f)
    ri_ref[...] = proj(6).astype(BF16)
    p = proj(7)
    sg_ref[...] = (p * sigmoid(p)).astype(BF16)


def _inproj(x2, nw, w_in, seg, qw, kw, lbf, lbb):
    n, d = x2.shape
    tm = TOKEN_TILE
    row = lambda i: (i, 0)
    out_bf = jax.ShapeDtypeStruct((n, GROUP_W), BF16)
    out_f = jax.ShapeDtypeStruct((n, GROUP_W), F32)
    ospec = pl.BlockSpec((tm, GROUP_W), row)
    return pl.pallas_call(
        _inproj_body,
        grid=(n // tm,),
        in_specs=[pl.BlockSpec((tm, d), row), _const_spec((1, d)), _const_spec(w_in.shape),
                  _const_spec(seg.shape)] + [_const_spec((1, GROUP_W))] * 4,
        out_specs=[ospec] * 10,
        out_shape=[out_bf, out_bf, out_bf, out_bf, out_bf, out_bf, out_f, out_f, out_bf, out_bf],
        compiler_params=_cparams(("parallel",)),
    )(x2, nw, w_in, seg, qw, kw, lbf, lbb)


def _attn_body(c_ref, lam_ref, qt_ref, k_ref, vt_ref, aug_ref, wcol_ref, o_ref, m_ref, acc_ref,
               *, bq, bk, nkv):
    hd = pl.program_id(1)
    t0 = pl.program_id(2) * bq
    c = c_ref[hd]
    lam = lam_ref[0]
    jd = lax.div(t0, bk)
    tg = t0 + lax.broadcasted_iota(jnp.int32, (1, bq), 1)

    m_ref[...] = jnp.full(m_ref.shape, -jnp.inf, F32)
    acc_ref[...] = jnp.zeros(acc_ref.shape, F32)

    def weights(direction):
        return [jnp.concatenate([qt_ref[mp], aug_ref[direction]], axis=0) for mp in range(2)]

    def step(j, qts, sign, bias):
        s0 = pl.multiple_of(j * bk, bk)
        w = (sign * c) * (tg - s0).astype(F32)
        vt = vt_ref[:, pl.ds(s0, bk)]
        for mp in range(2):
            st = jnp.dot(k_ref[mp, pl.ds(s0, bk), :], qts[mp], preferred_element_type=F32)
            if bias is not None:
                st = st + bias
            m_old = m_ref[mp]
            m_new = jnp.maximum(m_old, jnp.max(st, axis=0, keepdims=True) + w)
            alpha = jnp.exp2(m_old - m_new)
            p = jnp.exp2((st - (m_new - w)).astype(BF16))
            acc_ref[mp] = acc_ref[mp] * alpha + jnp.dot(vt, p, preferred_element_type=F32)
            m_ref[mp] = m_new

    q_left = weights(0)

    def left(j, carry):
        step(j, q_left, -1.0, None)
        return carry

    lax.fori_loop(0, jd, left, 0)

    rows = lax.broadcasted_iota(jnp.int32, (bk, bq), 0)
    dist = jnp.abs((tg - jd * bk) - rows).astype(F32)
    step(jd, weights(1), 0.0, dist * (-c))

    q_right = weights(2)

    def right(j, carry):
        step(j, q_right, 1.0, None)
        return carry

    lax.fori_loop(jd + 1, nkv, right, 0)

    a1 = acc_ref[0]
    a2 = acc_ref[1]
    o1 = a1[:ATTN_VDIM] * (1.0 / a1[ATTN_VDIM:ATTN_VDIM + 1])
    o2 = a2[:ATTN_VDIM] * (1.0 / a2[ATTN_VDIM:ATTN_VDIM + 1])
    o = o1 - lam * o2
    ms = jnp.mean(o * o, axis=0, keepdims=True)
    y = o * lax.rsqrt(ms + NORM_EPS) * wcol_ref[...] * (1.0 - LAM_INIT)
    o_ref[...] = y.T.astype(BF16)


def _attn(cs, lam, qt, kp, vt, aug, wcol):
    b, h, _, _, l = qt.shape
    bq, bk = ATTN_BQ, ATTN_BK
    body = functools.partial(_attn_body, bq=bq, bk=bk, nkv=l // bk)
    smem = pl.BlockSpec(memory_space=pltpu.SMEM)
    return pl.pallas_call(
        body,
        grid=(b, h, l // bq),
        in_specs=[smem, smem,
                  pl.BlockSpec((None, None, 2, ATTN_QKDIM, bq), lambda bi, hi, i: (bi, hi, 0, 0, i)),
                  pl.BlockSpec((None, None, 2, l, 128), lambda bi, hi, i: (bi, hi, 0, 0, 0)),
                  pl.BlockSpec((None, None, ATTN_VROWS, l), lambda bi, hi, i: (bi, hi, 0, 0)),
                  pl.BlockSpec((None, 3, ATTN_QKDIM, bq), lambda bi, hi, i: (hi, 0, 0, 0)),
                  _const_spec((ATTN_VDIM, bq))],
        out_specs=pl.BlockSpec((None, bq, ATTN_VDIM), lambda bi, hi, i: (bi, i, hi)),
        out_shape=jax.ShapeDtypeStruct((b, l, h * ATTN_VDIM), BF16),
        scratch_shapes=[pltpu.VMEM((2, 1, bq), F32), pltpu.VMEM((2, ATTN_VROWS, bq), F32)],
        compiler_params=_cparams(("parallel", "parallel", "arbitrary")),
    )(cs, lam, qt, kp, vt, aug, wcol)


def _gla_levels(c):
    lv, half = [], c // 2
    while half >= 8:
        lv.append(half)
        half //= 2
    return lv


def _gla_constants(c):
    r = np.arange(c)[:, None]
    u = np.arange(c)[None, :]
    mats = [(u <= r), (u > r)]
    bmasks = []
    for half in _gla_levels(c):
        blk = r // (2 * half)
        upper = (r % (2 * half)) >= half
        mid = blk * 2 * half + half - 1
        mats.append(np.where(upper, (u > mid) & (u <= r), (u > r) & (u <= mid)))
        t, s = r, u
        bmasks.append(((t // (2 * half)) == (s // (2 * half)))
                      & ((t % (2 * half)) >= half) & ((s % (2 * half)) < half))
    dmask = ((r // 8) == (u // 8)) & (u <= r)
    fwd_m = np.concatenate([m.astype(np.float32) for m in mats], axis=0)
    fwd_b = np.stack([m.astype(np.float32) for m in bmasks])
    fwd_d = dmask.astype(np.float32)
    flip = lambda m: m[::-1, ::-1]
    bwd_m = np.concatenate([flip(m.astype(np.float32)) for m in mats], axis=0)
    bwd_b = np.stack([flip(m.astype(np.float32)) for m in bmasks])
    bwd_d = flip(fwd_d)
    return (jnp.asarray(np.stack([fwd_m, bwd_m]), BF16),
            jnp.asarray(np.stack([fwd_b, bwd_b]), F32),
            jnp.asarray(np.stack([fwd_d, bwd_d]), F32))


def _gla_body(qf_ref, kf_ref, vf_ref, lf_ref, qb_ref, kb_ref, vb_ref, lb_ref,
              mst_ref, bm_ref, dm_ref, of_ref, ob_ref, s_ref, *, c, nch):
    nlev = len(_gla_levels(c))
    nt = (((1,), (1,)), ((), ()))

    @pl.when(pl.program_id(1) == 0)
    def _():
        s_ref[...] = jnp.zeros(s_ref.shape, F32)

    lane = lax.broadcasted_iota(jnp.int32, (8, c), 1)
    dirs = ((qf_ref, kf_ref, vf_ref, lf_ref, of_ref), (qb_ref, kb_ref, vb_ref, lb_ref, ob_ref))

    def chunk(ci, carry):
        for d, (q_ref, k_ref, v_ref, l_ref, o_ref) in enumerate(dirs):
            cc = ci if d == 0 else nch - 1 - ci
            r0 = pl.multiple_of(cc * c, c)
            lg = l_ref[pl.ds(r0, c), :]
            h1 = lg.astype(BF16)
            r1 = lg - h1.astype(F32)
            h2 = r1.astype(BF16)
            h3 = (r1 - h2.astype(F32)).astype(BF16)
            mst = mst_ref[d]
            ex = (jnp.dot(mst, h1, preferred_element_type=F32)
                  + jnp.dot(mst, h2, preferred_element_type=F32)
                  + jnp.dot(mst, h3, preferred_element_type=F32))
            for hd in range(REC_HEADS):
                cols = slice(hd * REC_DIM, (hd + 1) * REC_DIM)
                g = ex[0:c, cols]
                q = q_ref[pl.ds(r0, c), cols].astype(F32)
                k = k_ref[pl.ds(r0, c), cols].astype(F32)
                v = v_ref[pl.ds(r0, c), cols]
                st = s_ref[d, hd]
                o = lax.dot_general((q * jnp.exp(g)).astype(BF16), st.astype(BF16), nt,
                                    preferred_element_type=F32)
                a = jnp.zeros((c, c), F32)
                for li in range(nlev):
                    eh = jnp.exp(ex[(2 + li) * c:(3 + li) * c, cols])
                    pr = lax.dot_general((q * eh).astype(BF16), (k * eh).astype(BF16), nt,
                                         preferred_element_type=F32)
                    a = a + bm_ref[d, li] * pr
                strips = []
                for blk in range(c // 8):
                    rs = slice(8 * blk, 8 * blk + 8)
                    gb, qb, kb = g[rs], q[rs], k[rs]
                    strip = jnp.zeros((8, c), F32)
                    for s in range(8):
                        e = jnp.exp(jnp.minimum(gb - gb[s:s + 1], 0.0))
                        col = jnp.sum(qb * (kb[s:s + 1] * e), axis=-1, keepdims=True)
                        strip = jnp.where(lane == 8 * blk + s, col, strip)
                    strips.append(strip)
                a = a + dm_ref[d] * jnp.concatenate(strips, axis=0)
                o = o + jnp.dot(a.astype(BF16), v, preferred_element_type=F32)
                o_ref[pl.ds(r0, c), cols] = o
                kt = (k * jnp.exp(ex[c:2 * c, cols])).astype(BF16)
                g_all = g[c - 1:c] if d == 0 else g[0:1]
                s_ref[d, hd] = st * jnp.exp(g_all) + jnp.dot(
                    v.astype(F32).T.astype(BF16), kt, preferred_element_type=F32)
        return carry

    lax.fori_loop(0, nch, chunk, 0)


def _gla(rq, kf, kb, ri, lf, lb, consts):
    b, l, w = rq.shape
    c, tc = GLA_CHUNK, GLA_TILE
    n = l // tc
    mst, bm, dm = consts
    fwd = pl.BlockSpec((None, tc, w), lambda bi, i: (bi, i, 0))
    bwd = pl.BlockSpec((None, tc, w), lambda bi, i: (bi, n - 1 - i, 0))
    out = jax.ShapeDtypeStruct((b, l, w), F32)
    return pl.pallas_call(
        functools.partial(_gla_body, c=c, nch=tc // c),
        grid=(b, n),
        in_specs=[fwd, fwd, fwd, fwd, bwd, bwd, bwd, bwd,
                  _const_spec(mst.shape), _const_spec(bm.shape), _const_spec(dm.shape)],
        out_specs=[fwd, bwd],
        out_shape=[out, out],
        scratch_shapes=[pltpu.VMEM((2, REC_HEADS, REC_DIM, REC_DIM), F32)],
        compiler_params=_cparams(("parallel", "arbitrary")),
    )(rq, kf, ri, lf, rq, kb, ri, lb, mst, bm, dm)


def _outproj_body(x_ref, ao_ref, of_ref, ob_ref, sg_ref, rw_ref, wo_ref, fw_ref, x1_ref, h2_ref):
    ro = of_ref[...] + ob_ref[...]
    parts = []
    for hd in range(REC_HEADS):
        seg = ro[:, hd * REC_DIM:(hd + 1) * REC_DIM]
        ms = jnp.mean(seg * seg, axis=-1, keepdims=True)
        parts.append(seg * lax.rsqrt(ms + NORM_EPS))
    ron = jnp.concatenate(parts, axis=-1) * rw_ref[...] * sg_ref[...].astype(F32)
    aw = ao_ref.shape[-1]
    mix = (jnp.dot(ao_ref[...], wo_ref[:aw, :], preferred_element_type=F32)
           + jnp.dot(ron.astype(BF16), wo_ref[aw:, :], preferred_element_type=F32))
    x1 = x_ref[...] + mix
    x1_ref[...] = x1
    ms = jnp.mean(x1 * x1, axis=-1, keepdims=True)
    h2_ref[...] = (x1 * lax.rsqrt(ms + NORM_EPS) * fw_ref[...]).astype(BF16)


def _outproj(x2, ao, o_f, o_b, sg, rw, w_out, fw):
    n, d = x2.shape
    tm = TOKEN_TILE
    row = lambda i: (i, 0)
    half = pl.BlockSpec((tm, GROUP_W), row)
    full = pl.BlockSpec((tm, d), row)
    return pl.pallas_call(
        _outproj_body,
        grid=(n // tm,),
        in_specs=[full, half, half, half, half, _const_spec((1, GROUP_W)),
                  _const_spec(w_out.shape), _const_spec((1, d))],
        out_specs=[full, full],
        out_shape=[jax.ShapeDtypeStruct((n, d), F32), jax.ShapeDtypeStruct((n, d), BF16)],
        compiler_params=_cparams(("parallel",)),
    )(x2, ao, o_f, o_b, sg, rw, w_out, fw)


def _ffn_body(hp_ref, hc_ref, hn_ref, x1_ref, wu_ref, cw_ref, cb_ref, wd_ref, y_ref, acc_ref,
              *, tm, ft, dff, tiles_per_seq):
    i = pl.program_id(0)
    pos = lax.rem(i, tiles_per_seq)
    hp = jnp.where(pos == 0, jnp.zeros_like(hp_ref[...]), hp_ref[...])
    hn = jnp.where(pos == tiles_per_seq - 1, jnp.zeros_like(hn_ref[...]), hn_ref[...])
    hx = jnp.concatenate([hp, hc_ref[...], hn], axis=0)
    ext = tm + 2 * HALO
    acc_ref[...] = x1_ref[...]

    def conv(u, col):
        prev = pltpu.roll(u, 1, axis=0)[HALO:HALO + tm]
        nxt = pltpu.roll(u, ext - 1, axis=0)[HALO:HALO + tm]
        cw = cw_ref[:, pl.ds(col, ft)]
        return (cb_ref[:, pl.ds(col, ft)] + prev * cw[0:1] + u[HALO:HALO + tm] * cw[1:2]
                + nxt * cw[2:3])

    def tile(f, carry):
        cg = pl.multiple_of(f * ft, ft)
        cu = pl.multiple_of(dff + f * ft, ft)
        gate = conv(jnp.dot(hx, wu_ref[:, pl.ds(cg, ft)], preferred_element_type=F32), cg)
        up = conv(jnp.dot(hx, wu_ref[:, pl.ds(cu, ft)], preferred_element_type=F32), cu)
        act = (gate * (1.0 / (1.0 + jnp.exp(-gate))) * up).astype(BF16)
        acc_ref[...] += jnp.dot(act, wd_ref[pl.ds(cg, ft), :], preferred_element_type=F32)
        return carry

    lax.fori_loop(0, dff // ft, tile, 0)
    y_ref[...] = acc_ref[...]


def _ffn(h2, x1, w_up, conv_w, conv_b, w_down, seq_len):
    n, d = x1.shape
    tm, ft = TOKEN_TILE, FFN_FT
    dff = w_down.shape[0]
    nt = n // tm
    hb = tm // HALO
    nhb = n // HALO
    row = lambda i: (i, 0)
    body = functools.partial(_ffn_body, tm=tm, ft=ft, dff=dff, tiles_per_seq=seq_len // tm)
    return pl.pallas_call(
        body,
        grid=(nt,),
        in_specs=[pl.BlockSpec((HALO, d), lambda i: (jnp.maximum(i * hb - 1, 0), 0)),
                  pl.BlockSpec((tm, d), row),
                  pl.BlockSpec((HALO, d), lambda i: (jnp.minimum((i + 1) * hb, nhb - 1), 0)),
                  pl.BlockSpec((tm, d), row),
                  _const_spec(w_up.shape), _const_spec(conv_w.shape), _const_spec(conv_b.shape),
                  _const_spec(w_down.shape)],
        out_specs=pl.BlockSpec((tm, d), row),
        out_shape=jax.ShapeDtypeStruct((n, d), F32),
        scratch_shapes=[pltpu.VMEM((tm, d), F32)],
        compiler_params=_cparams(("parallel",)),
    )(h2, h2, h2, x1, w_up, conv_w, conv_b, w_down)


def _bf16_split3(x):
    x = x.astype(F32)
    a = x.astype(BF16)
    r = x - a.astype(F32)
    b = r.astype(BF16)
    c = (r - b.astype(F32)).astype(BF16)
    return a, b, c


def _alibi_operands(l):
    start = 2.0 ** (-8.0 / ATTN_HEADS)
    slopes = np.array([start ** (i + 1) for i in range(ATTN_HEADS)], np.float32)
    cs = jnp.asarray(slopes * np.float32(LOG2E), F32)
    pos = np.arange(l) % ATTN_BK
    kcols = np.zeros((l, 128 - ATTN_QKDIM), np.float32)
    kcols[:, 0:3] = (pos - pos % 16)[:, None]
    kcols[:, 3:6] = (pos % 16)[:, None]
    c1, c2, c3 = _bf16_split3(cs)
    six = jnp.stack([c1, c2, c3, c1, c2, c3], axis=1).astype(F32)
    rows = jnp.zeros((ATTN_HEADS, ATTN_QKDIM), F32).at[:, :6].set(six)
    aug = jnp.stack([rows, jnp.zeros_like(rows), -rows], axis=1)
    aug = jnp.broadcast_to(aug[..., None], aug.shape + (ATTN_BQ,)).astype(BF16)
    return cs, jnp.asarray(kcols, BF16), aug


def _trunk(x, norm_mix_w, w_in, q_norm_w, k_norm_w, lam, attn_out_norm_w, lb_fwd, lb_bwd,
           rec_out_norm_w, w_out, norm_ffn_w, w_up, conv_w, conv_b, w_down):
    b, l, d = x.shape
    n = b * l
    x2 = x.reshape(n, d)

    seg = np.kron(np.eye(GROUP_W // ATTN_QKDIM, dtype=np.float32),
                  np.full((ATTN_QKDIM, ATTN_QKDIM), 1.0 / ATTN_QKDIM, np.float32))
    reps = GROUP_W // ATTN_QKDIM
    lbf = jnp.cumsum(jax.nn.softmax(lb_fwd.astype(F32), axis=0), axis=0)[0][None]
    lbb = jnp.cumsum(jax.nn.softmax(lb_bwd.astype(F32), axis=0), axis=0)[0][None]
    q, k, v, rq, kf, kb, lf, lb, ri, sg = _inproj(
        x2, norm_mix_w[None], w_in.astype(BF16), jnp.asarray(seg, BF16),
        jnp.tile(q_norm_w, reps)[None], jnp.tile(k_norm_w, reps)[None], lbf, lbb)

    cs, kcols, aug = _alibi_operands(l)
    qt = q.reshape(b, l, ATTN_HEADS, 2, ATTN_QKDIM).transpose(0, 2, 3, 4, 1)
    kh = k.reshape(b, l, ATTN_HEADS, 2, ATTN_QKDIM).transpose(0, 2, 3, 1, 4)
    kp = jnp.concatenate(
        [kh, jnp.broadcast_to(kcols, (b, ATTN_HEADS, 2, l, kcols.shape[-1]))], axis=-1)
    vt = v.reshape(b, l, ATTN_HEADS, ATTN_VDIM).transpose(0, 2, 3, 1)
    ones = jnp.zeros((b, ATTN_HEADS, ATTN_VROWS - ATTN_VDIM, l), BF16).at[:, :, 0].set(1.0)
    vt = jnp.concatenate([vt, ones], axis=2)
    wcol = jnp.broadcast_to(attn_out_norm_w.astype(F32)[:, None], (ATTN_VDIM, ATTN_BQ))
    ao = _attn(cs, lam.reshape(1), qt, kp, vt, aug, wcol)

    shape3 = (b, l, GROUP_W)
    o_f, o_b = _gla(rq.reshape(shape3), kf.reshape(shape3), kb.reshape(shape3),
                    ri.reshape(shape3), lf.reshape(shape3), lb.reshape(shape3),
                    _gla_constants(GLA_CHUNK))

    x1, h2 = _outproj(x2, ao.reshape(n, GROUP_W), o_f.reshape(n, GROUP_W),
                      o_b.reshape(n, GROUP_W), sg, jnp.tile(rec_out_norm_w, REC_HEADS)[None],
                      w_out.astype(BF16), norm_ffn_w[None])
    y = _ffn(h2, x1, w_up.astype(BF16), conv_w, conv_b[None], w_down.astype(BF16), l)
    return y.reshape(b, l, d)


def kernel(x_prompt, x_sample, norm_mix_w, w_in, q_norm_w, k_norm_w, lambda_q1, lambda_k1,
           lambda_q2, lambda_k2, attn_out_norm_w, lb_fwd, lb_bwd, rec_out_norm_w, w_out,
           norm_ffn_w, w_up, conv_w, conv_b, w_down):
    assert norm_mix_w.shape[0] == 1, "single-layer trunk"
    lam = (jnp.exp(jnp.sum(lambda_q1[0].astype(F32) * lambda_k1[0].astype(F32)))
           - jnp.exp(jnp.sum(lambda_q2[0].astype(F32) * lambda_k2[0].astype(F32))) + LAM_INIT)
    params = (norm_mix_w[0], w_in[0], q_norm_w[0], k_norm_w[0], lam, attn_out_norm_w[0],
              lb_fwd, lb_bwd, rec_out_norm_w[0], w_out[0], norm_ffn_w[0], w_up[0], conv_w[0],
              conv_b[0], w_down[0])
    return (_trunk(x_prompt, *params), _trunk(x_sample, *params))
```

```python
import functools
import math

import numpy as np
import jax
import jax.numpy as jnp
from jax import lax
from jax.experimental import pallas as pl
from jax.experimental.pallas import tpu as pltpu

F32 = jnp.float32
BF16 = jnp.bfloat16

NORM_EPS = 1e-6
ATTN_HEADS = 4
ATTN_QKDIM = 64
ATTN_VDIM = 128
REC_HEADS = 4
REC_DIM = 128
GROUP_W = 512
LOG2E = 1.4426950408889634
LAM_INIT = 0.8 - 0.6 * math.exp(-0.3 * 0)

V7X_VMEM_LIMIT = 56 * 1024 * 1024

TOKEN_TILE = 512
ATTN_BQ = 256
ATTN_BK = 512
ATTN_MAX_GROUP = 15
ATTN_LOOKAHEAD = 4
ATTN_VROWS = ATTN_VDIM + 8
GLA_CHUNK = 128
GLA_TILE = 512
FFN_FT = 256
HALO = 16


def _cparams(sem):
    return pltpu.CompilerParams(dimension_semantics=sem, vmem_limit_bytes=V7X_VMEM_LIMIT)


def _const_spec(shape):
    zeros = (0,) * len(shape)
    return pl.BlockSpec(shape, lambda *_: zeros)


def _inproj_body(x_ref, nw_ref, w_ref, seg_ref, qw_ref, kw_ref, lbf_ref, lbb_ref,
                 q_ref, k_ref, v_ref, rq_ref, kf_ref, kb_ref, lf_ref, lb_ref, ri_ref, sg_ref):
    x = x_ref[...]
    ms = jnp.mean(x * x, axis=-1, keepdims=True)
    h = (x * lax.rsqrt(ms + NORM_EPS) * nw_ref[...]).astype(BF16)

    def proj(g):
        return jnp.dot(h, w_ref[:, g * GROUP_W:(g + 1) * GROUP_W], preferred_element_type=F32)

    def head_norm(p, w):
        p2 = p * p
        hi = p2.astype(BF16)
        lo = (p2 - hi.astype(F32)).astype(BF16)
        seg = seg_ref[...]
        m = (jnp.dot(hi, seg, preferred_element_type=F32)
             + jnp.dot(lo, seg, preferred_element_type=F32))
        return p * lax.rsqrt(m + NORM_EPS) * w

    def sigmoid(p):
        return 1.0 / (1.0 + jnp.exp(-p))

    q_ref[...] = (head_norm(proj(0), qw_ref[...]) * (ATTN_QKDIM ** -0.5 * LOG2E)).astype(BF16)
    k_ref[...] = head_norm(proj(1), kw_ref[...]).astype(BF16)
    v_ref[...] = proj(2).astype(BF16)
    p = proj(3)
    rq_ref[...] = (p * sigmoid(p)).astype(BF16)
    for g, lbr, kr, lr in ((4, lbf_ref, kf_ref, lf_ref), (5, lbb_ref, kb_ref, lb_ref)):
        lbv = lbr[...]
        f = lbv + (1.0 - lbv) * sigmoid(proj(g))
        kr[...] = (1.0 - f).astype(BF16)
        lr[...] = jnp.log(f)
    ri_ref[...] = proj(6).astype(BF16)
    p = proj(7)
    sg_ref[...] = (p * sigmoid(p)).astype(BF16)


def _inproj(x2, nw, w_in, seg, qw, kw, lbf, lbb):
    n, d = x2.shape
    tm = TOKEN_TILE
    row = lambda i: (i, 0)
    out_bf = jax.ShapeDtypeStruct((n, GROUP_W), BF16)
    out_f = jax.ShapeDtypeStruct((n, GROUP_W), F32)
    ospec = pl.BlockSpec((tm, GROUP_W), row)
    return pl.pallas_call(
        _inproj_body,
        grid=(n // tm,),
        in_specs=[pl.BlockSpec((tm, d), row), _const_spec((1, d)), _const_spec(w_in.shape),
                  _const_spec(seg.shape)] + [_const_spec((1, GROUP_W))] * 4,
        out_specs=[ospec] * 10,
        out_shape=[out_bf, out_bf, out_bf, out_bf, out_bf, out_bf, out_f, out_f, out_bf, out_bf],
        compiler_params=_cparams(("parallel",)),
    )(x2, nw, w_in, seg, qw, kw, lbf, lbb)


def _attn_body(c_ref, lam_ref, qt_ref, k_ref, vt_ref, aug_ref, wcol_ref, o_ref, m_ref, acc_ref,
               *, bq, bk, nkv, group_size):
    hd = pl.program_id(1)
    t0 = pl.program_id(2) * bq
    c = c_ref[hd]
    lam = lam_ref[0]
    jd = lax.div(t0, bk)
    tg = t0 + lax.broadcasted_iota(jnp.int32, (1, bq), 1)

    m_ref[...] = jnp.full(m_ref.shape, -jnp.inf, F32)
    acc_ref[...] = jnp.zeros(acc_ref.shape, F32)

    def scores(blk, mp):
        j, aug, _, bias = blk
        s0 = pl.multiple_of(j * bk, bk)
        qt = jnp.concatenate([qt_ref[mp], aug], axis=0)
        st = jnp.dot(k_ref[mp, pl.ds(s0, bk), :], qt, preferred_element_type=F32)
        return st if bias is None else st + bias

    def fold(blk, mp, st):
        j, _, wcoef, _ = blk
        s0 = pl.multiple_of(j * bk, bk)
        w = wcoef * (tg - s0).astype(F32)
        m_old = m_ref[mp]
        m_new = jnp.maximum(m_old, jnp.max(st, axis=0, keepdims=True) + w)
        alpha = jnp.exp2(m_old - m_new)
        p = jnp.exp2((st - (m_new - w)).astype(BF16))
        pv = jnp.dot(vt_ref[:, pl.ds(s0, bk)], p, preferred_element_type=F32)
        acc_ref[mp] = acc_ref[mp] * alpha + pv
        m_ref[mp] = m_new

    def run(blocks):
        units = [(blk, mp) for blk in blocks for mp in range(2)]
        pending = [scores(*u) for u in units[:ATTN_LOOKAHEAD]]
        for n, unit in enumerate(units):
            if n + ATTN_LOOKAHEAD < len(units):
                pending.append(scores(*units[n + ATTN_LOOKAHEAD]))
            fold(*unit, pending.pop(0))

    rows = lax.broadcasted_iota(jnp.int32, (bk, bq), 0)
    dist = jnp.abs((tg - jd * bk) - rows).astype(F32)
    diag = (jd, jnp.zeros((ATTN_QKDIM, bq), BF16), 0.0, dist * (-c))

    def group_blocks(gi):
        blocks = []
        for u in range(group_size):
            idx = gi * group_size + u
            j = idx + (idx >= jd).astype(jnp.int32)
            sgn = jnp.where(j < jd, 1.0, -1.0)
            blocks.append((j, (aug_ref[...] * sgn).astype(BF16), -sgn * c, None))
        return blocks

    def group(gi, carry):
        run(group_blocks(gi))
        return carry

    run([diag] + group_blocks(0))
    lax.fori_loop(1, (nkv - 1) // group_size, group, 0)

    a1 = acc_ref[0]
    a2 = acc_ref[1]
    o1 = a1[:ATTN_VDIM] * (1.0 / a1[ATTN_VDIM:ATTN_VDIM + 1])
    o2 = a2[:ATTN_VDIM] * (1.0 / a2[ATTN_VDIM:ATTN_VDIM + 1])
    o = o1 - lam * o2
    ms = jnp.mean(o * o, axis=0, keepdims=True)
    y = o * lax.rsqrt(ms + NORM_EPS) * wcol_ref[...] * (1.0 - LAM_INIT)
    o_ref[...] = y.T.astype(BF16)


def _attn(cs, lam, qt, kp, vt, aug, wcol):
    b, h, _, _, l = qt.shape
    bq, bk = ATTN_BQ, ATTN_BK
    nkv = l // bk
    group_size = max(g for g in range(1, ATTN_MAX_GROUP + 1) if (nkv - 1) % g == 0)
    body = functools.partial(_attn_body, bq=bq, bk=bk, nkv=nkv, group_size=group_size)
    smem = pl.BlockSpec(memory_space=pltpu.SMEM)
    return pl.pallas_call(
        body,
        grid=(b, h, l // bq),
        in_specs=[smem, smem,
                  pl.BlockSpec((None, None, 2, ATTN_QKDIM, bq), lambda bi, hi, i: (bi, hi, 0, 0, i)),
                  pl.BlockSpec((None, None, 2, l, 128), lambda bi, hi, i: (bi, hi, 0, 0, 0)),
                  pl.BlockSpec((None, None, ATTN_VROWS, l), lambda bi, hi, i: (bi, hi, 0, 0)),
                  pl.BlockSpec((None, ATTN_QKDIM, bq), lambda bi, hi, i: (hi, 0, 0)),
                  _const_spec((ATTN_VDIM, bq))],
        out_specs=pl.BlockSpec((None, bq, ATTN_VDIM), lambda bi, hi, i: (bi, i, hi)),
        out_shape=jax.ShapeDtypeStruct((b, l, h * ATTN_VDIM), BF16),
        scratch_shapes=[pltpu.VMEM((2, 1, bq), F32), pltpu.VMEM((2, ATTN_VROWS, bq), F32)],
        compiler_params=_cparams(("parallel", "parallel", "arbitrary")),
    )(cs, lam, qt, kp, vt, aug, wcol)


def _gla_levels(c):
    lv, half = [], c // 2
    while half >= 8:
        lv.append(half)
        half //= 2
    return lv


def _gla_constants(c):
    r = np.arange(c)[:, None]
    u = np.arange(c)[None, :]
    mats = [(u <= r), (u > r)]
    bmasks = []
    for half in _gla_levels(c):
        blk = r // (2 * half)
        upper = (r % (2 * half)) >= half
        mid = blk * 2 * half + half - 1
        mats.append(np.where(upper, (u > mid) & (u <= r), (u > r) & (u <= mid)))
        t, s = r, u
        bmasks.append(((t // (2 * half)) == (s // (2 * half)))
                      & ((t % (2 * half)) >= half) & ((s % (2 * half)) < half))
    dmask = ((r // 8) == (u // 8)) & (u <= r)
    fwd_m = np.concatenate([m.astype(np.float32) for m in mats], axis=0)
    fwd_b = np.stack([m.astype(np.float32) for m in bmasks])
    fwd_d = dmask.astype(np.float32)
    flip = lambda m: m[::-1, ::-1]
    bwd_m = np.concatenate([flip(m.astype(np.float32)) for m in mats], axis=0)
    bwd_b = np.stack([flip(m.astype(np.float32)) for m in bmasks])
    bwd_d = flip(fwd_d)
    return (jnp.asarray(np.stack([fwd_m, bwd_m]), BF16),
            jnp.asarray(np.stack([fwd_b, bwd_b]), F32),
            jnp.asarray(np.stack([fwd_d, bwd_d]), F32))


def _gla_body(qf_ref, kf_ref, vf_ref, lf_ref, qb_ref, kb_ref, vb_ref, lb_ref,
              mst_ref, bm_ref, dm_ref, of_ref, ob_ref, s_ref, *, c, nch):
    nlev = len(_gla_levels(c))
    nt = (((1,), (1,)), ((), ()))

    @pl.when(pl.program_id(1) == 0)
    def _():
        s_ref[...] = jnp.zeros(s_ref.shape, F32)

    lane = lax.broadcasted_iota(jnp.int32, (8, c), 1)
    dirs = ((qf_ref, kf_ref, vf_ref, lf_ref, of_ref), (qb_ref, kb_ref, vb_ref, lb_ref, ob_ref))

    def chunk(ci, carry):
        for d, (q_ref, k_ref, v_ref, l_ref, o_ref) in enumerate(dirs):
            cc = ci if d == 0 else nch - 1 - ci
            r0 = pl.multiple_of(cc * c, c)
            lg = l_ref[pl.ds(r0, c), :]
            h1 = lg.astype(BF16)
            r1 = lg - h1.astype(F32)
            h2 = r1.astype(BF16)
            h3 = (r1 - h2.astype(F32)).astype(BF16)
            mst = mst_ref[d]
            ex = (jnp.dot(mst, h1, preferred_element_type=F32)
                  + jnp.dot(mst, h2, preferred_element_type=F32)
                  + jnp.dot(mst, h3, preferred_element_type=F32))
            for hd in range(REC_HEADS):
                cols = slice(hd * REC_DIM, (hd + 1) * REC_DIM)
                g = ex[0:c, cols]
                q = q_ref[pl.ds(r0, c), cols].astype(F32)
                k = k_ref[pl.ds(r0, c), cols].astype(F32)
                v = v_ref[pl.ds(r0, c), cols]
                st = s_ref[d, hd]
                o = lax.dot_general((q * jnp.exp(g)).astype(BF16), st.astype(BF16), nt,
                                    preferred_element_type=F32)
                a = jnp.zeros((c, c), F32)
                for li in range(nlev):
                    eh = jnp.exp(ex[(2 + li) * c:(3 + li) * c, cols])
                    pr = lax.dot_general((q * eh).astype(BF16), (k * eh).astype(BF16), nt,
                                         preferred_element_type=F32)
                    a = a + bm_ref[d, li] * pr
                strips = []
                for blk in range(c // 8):
                    rs = slice(8 * blk, 8 * blk + 8)
                    gb, qb, kb = g[rs], q[rs], k[rs]
                    strip = jnp.zeros((8, c), F32)
                    for s in range(8):
                        e = jnp.exp(jnp.minimum(gb - gb[s:s + 1], 0.0))
                        col = jnp.sum(qb * (kb[s:s + 1] * e), axis=-1, keepdims=True)
                        strip = jnp.where(lane == 8 * blk + s, col, strip)
                    strips.append(strip)
                a = a + dm_ref[d] * jnp.concatenate(strips, axis=0)
                o = o + jnp.dot(a.astype(BF16), v, preferred_element_type=F32)
                o_ref[pl.ds(r0, c), cols] = o
                kt = (k * jnp.exp(ex[c:2 * c, cols])).astype(BF16)
                g_all = g[c - 1:c] if d == 0 else g[0:1]
                s_ref[d, hd] = st * jnp.exp(g_all) + jnp.dot(
                    v.astype(F32).T.astype(BF16), kt, preferred_element_type=F32)
        return carry

    lax.fori_loop(0, nch, chunk, 0)


def _gla(rq, kf, kb, ri, lf, lb, consts):
    b, l, w = rq.shape
    c, tc = GLA_CHUNK, GLA_TILE
    n = l // tc
    mst, bm, dm = consts
    fwd = pl.BlockSpec((None, tc, w), lambda bi, i: (bi, i, 0))
    bwd = pl.BlockSpec((None, tc, w), lambda bi, i: (bi, n - 1 - i, 0))
    out = jax.ShapeDtypeStruct((b, l, w), F32)
    return pl.pallas_call(
        functools.partial(_gla_body, c=c, nch=tc // c),
        grid=(b, n),
        in_specs=[fwd, fwd, fwd, fwd, bwd, bwd, bwd, bwd,
                  _const_spec(mst.shape), _const_spec(bm.shape), _const_spec(dm.shape)],
        out_specs=[fwd, bwd],
        out_shape=[out, out],
        scratch_shapes=[pltpu.VMEM((2, REC_HEADS, REC_DIM, REC_DIM), F32)],
        compiler_params=_cparams(("parallel", "arbitrary")),
    )(rq, kf, ri, lf, rq, kb, ri, lb, mst, bm, dm)


def _outproj_body(x_ref, ao_ref, of_ref, ob_ref, sg_ref, rw_ref, wo_ref, fw_ref, x1_ref, h2_ref):
    ro = of_ref[...] + ob_ref[...]
    parts = []
    for hd in range(REC_HEADS):
        seg = ro[:, hd * REC_DIM:(hd + 1) * REC_DIM]
        ms = jnp.mean(seg * seg, axis=-1, keepdims=True)
        parts.append(seg * lax.rsqrt(ms + NORM_EPS))
    ron = jnp.concatenate(parts, axis=-1) * rw_ref[...] * sg_ref[...].astype(F32)
    aw = ao_ref.shape[-1]
    mix = (jnp.dot(ao_ref[...], wo_ref[:aw, :], preferred_element_type=F32)
           + jnp.dot(ron.astype(BF16), wo_ref[aw:, :], preferred_element_type=F32))
    x1 = x_ref[...] + mix
    x1_ref[...] = x1
    ms = jnp.mean(x1 * x1, axis=-1, keepdims=True)
    h2_ref[...] = (x1 * lax.rsqrt(ms + NORM_EPS) * fw_ref[...]).astype(BF16)


def _outproj(x2, ao, o_f, o_b, sg, rw, w_out, fw):
    n, d = x2.shape
    tm = TOKEN_TILE
    row = lambda i: (i, 0)
    half = pl.BlockSpec((tm, GROUP_W), row)
    full = pl.BlockSpec((tm, d), row)
    return pl.pallas_call(
        _outproj_body,
        grid=(n // tm,),
        in_specs=[full, half, half, half, half, _const_spec((1, GROUP_W)),
                  _const_spec(w_out.shape), _const_spec((1, d))],
        out_specs=[full, full],
        out_shape=[jax.ShapeDtypeStruct((n, d), F32), jax.ShapeDtypeStruct((n, d), BF16)],
        compiler_params=_cparams(("parallel",)),
    )(x2, ao, o_f, o_b, sg, rw, w_out, fw)


def _ffn_body(hp_ref, hc_ref, hn_ref, x1_ref, wu_ref, cw_ref, cb_ref, wd_ref, y_ref, acc_ref,
              *, tm, ft, dff, tiles_per_seq):
    i = pl.program_id(0)
    pos = lax.rem(i, tiles_per_seq)
    hp = jnp.where(pos == 0, jnp.zeros_like(hp_ref[...]), hp_ref[...])
    hn = jnp.where(pos == tiles_per_seq - 1, jnp.zeros_like(hn_ref[...]), hn_ref[...])
    hx = jnp.concatenate([hp, hc_ref[...], hn], axis=0)
    ext = tm + 2 * HALO
    acc_ref[...] = x1_ref[...]

    def conv(u, col):
        prev = pltpu.roll(u, 1, axis=0)[HALO:HALO + tm]
        nxt = pltpu.roll(u, ext - 1, axis=0)[HALO:HALO + tm]
        cw = cw_ref[:, pl.ds(col, ft)]
        return (cb_ref[:, pl.ds(col, ft)] + prev * cw[0:1] + u[HALO:HALO + tm] * cw[1:2]
                + nxt * cw[2:3])

    def tile(f, carry):
        cg = pl.multiple_of(f * ft, ft)
        cu = pl.multiple_of(dff + f * ft, ft)
        gate = conv(jnp.dot(hx, wu_ref[:, pl.ds(cg, ft)], preferred_element_type=F32), cg)
        up = conv(jnp.dot(hx, wu_ref[:, pl.ds(cu, ft)], preferred_element_type=F32), cu)
        act = (gate * (1.0 / (1.0 + jnp.exp(-gate))) * up).astype(BF16)
        acc_ref[...] += jnp.dot(act, wd_ref[pl.ds(cg, ft), :], preferred_element_type=F32)
        return carry

    lax.fori_loop(0, dff // ft, tile, 0)
    y_ref[...] = acc_ref[...]


def _ffn(h2, x1, w_up, conv_w, conv_b, w_down, seq_len):
    n, d = x1.shape
    tm, ft = TOKEN_TILE, FFN_FT
    dff = w_down.shape[0]
    nt = n // tm
    hb = tm // HALO
    nhb = n // HALO
    row = lambda i: (i, 0)
    body = functools.partial(_ffn_body, tm=tm, ft=ft, dff=dff, tiles_per_seq=seq_len // tm)
    return pl.pallas_call(
        body,
        grid=(nt,),
        in_specs=[pl.BlockSpec((HALO, d), lambda i: (jnp.maximum(i * hb - 1, 0), 0)),
                  pl.BlockSpec((tm, d), row),
                  pl.BlockSpec((HALO, d), lambda i: (jnp.minimum((i + 1) * hb, nhb - 1), 0)),
                  pl.BlockSpec((tm, d), row),
                  _const_spec(w_up.shape), _const_spec(conv_w.shape), _const_spec(conv_b.shape),
                  _const_spec(w_down.shape)],
        out_specs=pl.BlockSpec((tm, d), row),
        out_shape=jax.ShapeDtypeStruct((n, d), F32),
        scratch_shapes=[pltpu.VMEM((tm, d), F32)],
        compiler_params=_cparams(("parallel",)),
    )(h2, h2, h2, x1, w_up, conv_w, conv_b, w_down)


def _bf16_split3(x):
    x = x.astype(F32)
    a = x.astype(BF16)
    r = x - a.astype(F32)
    b = r.astype(BF16)
    c = (r - b.astype(F32)).astype(BF16)
    return a, b, c


def _alibi_operands(l):
    start = 2.0 ** (-8.0 / ATTN_HEADS)
    slopes = np.array([start ** (i + 1) for i in range(ATTN_HEADS)], np.float32)
    cs = jnp.asarray(slopes * np.float32(LOG2E), F32)
    pos = np.arange(l) % ATTN_BK
    kcols = np.zeros((l, 128 - ATTN_QKDIM), np.float32)
    kcols[:, 0:3] = (pos - pos % 16)[:, None]
    kcols[:, 3:6] = (pos % 16)[:, None]
    c1, c2, c3 = _bf16_split3(cs)
    six = jnp.stack([c1, c2, c3, c1, c2, c3], axis=1).astype(F32)
    rows = jnp.zeros((ATTN_HEADS, ATTN_QKDIM), F32).at[:, :6].set(six)
    aug = jnp.broadcast_to(rows[..., None], rows.shape + (ATTN_BQ,))
    return cs, jnp.asarray(kcols, BF16), aug


def _trunk(x, norm_mix_w, w_in, q_norm_w, k_norm_w, lam, attn_out_norm_w, lb_fwd, lb_bwd,
           rec_out_norm_w, w_out, norm_ffn_w, w_up, conv_w, conv_b, w_down):
    b, l, d = x.shape
    n = b * l
    x2 = x.reshape(n, d)

    seg = np.kron(np.eye(GROUP_W // ATTN_QKDIM, dtype=np.float32),
                  np.full((ATTN_QKDIM, ATTN_QKDIM), 1.0 / ATTN_QKDIM, np.float32))
    reps = GROUP_W // ATTN_QKDIM
    lbf = jnp.cumsum(jax.nn.softmax(lb_fwd.astype(F32), axis=0), axis=0)[0][None]
    lbb = jnp.cumsum(jax.nn.softmax(lb_bwd.astype(F32), axis=0), axis=0)[0][None]
    q, k, v, rq, kf, kb, lf, lb, ri, sg = _inproj(
        x2, norm_mix_w[None], w_in.astype(BF16), jnp.asarray(seg, BF16),
        jnp.tile(q_norm_w, reps)[None], jnp.tile(k_norm_w, reps)[None], lbf, lbb)

    cs, kcols, aug = _alibi_operands(l)
    qt = q.reshape(b, l, ATTN_HEADS, 2, ATTN_QKDIM).transpose(0, 2, 3, 4, 1)
    kh = k.reshape(b, l, ATTN_HEADS, 2, ATTN_QKDIM).transpose(0, 2, 3, 1, 4)
    kp = jnp.concatenate(
        [kh, jnp.broadcast_to(kcols, (b, ATTN_HEADS, 2, l, kcols.shape[-1]))], axis=-1)
    vt = v.reshape(b, l, ATTN_HEADS, ATTN_VDIM).transpose(0, 2, 3, 1)
    ones = jnp.zeros((b, ATTN_HEADS, ATTN_VROWS - ATTN_VDIM, l), BF16).at[:, :, 0].set(1.0)
    vt = jnp.concatenate([vt, ones], axis=2)
    wcol = jnp.broadcast_to(attn_out_norm_w.astype(F32)[:, None], (ATTN_VDIM, ATTN_BQ))
    ao = _attn(cs, lam.reshape(1), qt, kp, vt, aug, wcol)

    shape3 = (b, l, GROUP_W)
    o_f, o_b = _gla(rq.reshape(shape3), kf.reshape(shape3), kb.reshape(shape3),
                    ri.reshape(shape3), lf.reshape(shape3), lb.reshape(shape3),
                    _gla_constants(GLA_CHUNK))

    x1, h2 = _outproj(x2, ao.reshape(n, GROUP_W), o_f.reshape(n, GROUP_W),
                      o_b.reshape(n, GROUP_W), sg, jnp.tile(rec_out_norm_w, REC_HEADS)[None],
                      w_out.astype(BF16), norm_ffn_w[None])
    y = _ffn(h2, x1, w_up.astype(BF16), conv_w, conv_b[None], w_down.astype(BF16), l)
    return y.reshape(b, l, d)


def kernel(x_prompt, x_sample, norm_mix_w, w_in, q_norm_w, k_norm_w, lambda_q1, lambda_k1,
           lambda_q2, lambda_k2, attn_out_norm_w, lb_fwd, lb_bwd, rec_out_norm_w, w_out,
           norm_ffn_w, w_up, conv_w, conv_b, w_down):
    assert norm_mix_w.shape[0] == 1, "single-layer trunk"
    lam = (jnp.exp(jnp.sum(lambda_q1[0].astype(F32) * lambda_k1[0].astype(F32)))
           - jnp.exp(jnp.sum(lambda_q2[0].astype(F32) * lambda_k2[0].astype(F32))) + LAM_INIT)
    params = (norm_mix_w[0], w_in[0], q_norm_w[0], k_norm_w[0], lam, attn_out_norm_w[0],
              lb_fwd, lb_bwd, rec_out_norm_w[0], w_out[0], norm_ffn_w[0], w_up[0], conv_w[0],
              conv_b[0], w_down[0])
    return (_trunk(x_prompt, *params), _trunk(x_sample, *params))
```

```python
import functools
import math

import numpy as np
import jax
import jax.numpy as jnp
from jax import lax
from jax.experimental import pallas as pl
from jax.experimental.pallas import tpu as pltpu

F32 = jnp.float32
BF16 = jnp.bfloat16

NORM_EPS = 1e-6
ATTN_HEADS = 4
ATTN_QKDIM = 64
ATTN_VDIM = 128
REC_HEADS = 4
REC_DIM = 128
GROUP_W = 512
LOG2E = 1.4426950408889634
LAM_INIT = 0.8 - 0.6 * math.exp(-0.3 * 0)

V7X_VMEM_LIMIT = 56 * 1024 * 1024

TOKEN_TILE = 512
ATTN_BQ = 256
ATTN_BK = 512
ATTN_MAX_GROUP = 15
ATTN_LOOKAHEAD = 4
ATTN_VROWS = ATTN_VDIM + 16
GLA_CHUNK = 128
GLA_TILE = 512
FFN_FT = 256
HALO = 16


def _cparams(sem):
    return pltpu.CompilerParams(dimension_semantics=sem, vmem_limit_bytes=V7X_VMEM_LIMIT)


def _const_spec(shape):
    zeros = (0,) * len(shape)
    return pl.BlockSpec(shape, lambda *_: zeros)


def _inproj_body(x_ref, nw_ref, w_ref, seg_ref, qw_ref, kw_ref, lbf_ref, lbb_ref,
                 kaug_ref, q_ref, k_ref, v_ref, rq_ref, kf_ref, kb_ref, lf_ref, lb_ref, ri_ref,
                 sg_ref):
    x = x_ref[...]
    ms = jnp.mean(x * x, axis=-1, keepdims=True)
    h = (x * lax.rsqrt(ms + NORM_EPS) * nw_ref[...]).astype(BF16)

    def proj(g):
        return jnp.dot(h, w_ref[:, g * GROUP_W:(g + 1) * GROUP_W], preferred_element_type=F32)

    def head_norm(p, w):
        m = jnp.dot((p * p).astype(BF16), seg_ref[...], preferred_element_type=F32)
        return p * lax.rsqrt(m + NORM_EPS) * w

    def sigmoid(p):
        return 1.0 / (1.0 + jnp.exp(-p))

    qn = head_norm(proj(0), qw_ref[...]) * (ATTN_QKDIM ** -0.5 * LOG2E)
    q_ref[...] = qn.T.astype(BF16)
    kn = head_norm(proj(1), kw_ref[...])
    low = lax.broadcasted_iota(jnp.int32, (kn.shape[0], 128), 1) < ATTN_QKDIM
    for t in range(GROUP_W // 128):
        pair = kn[:, t * 128:(t + 1) * 128]
        for half, src in ((0, pair), (1, pltpu.roll(pair, ATTN_QKDIM, axis=1))):
            cols = slice((2 * t + half) * 128, (2 * t + half + 1) * 128)
            k_ref[:, cols] = jnp.where(low, src, kaug_ref[:, cols]).astype(BF16)
    vt = proj(2).T
    tm = vt.shape[-1]
    tail = (lax.broadcasted_iota(jnp.int32, (ATTN_VROWS - ATTN_VDIM, tm), 0) == 0)
    for hd in range(ATTN_HEADS):
        v_ref[hd, :ATTN_VDIM, :] = vt[hd * ATTN_VDIM:(hd + 1) * ATTN_VDIM].astype(BF16)
        v_ref[hd, ATTN_VDIM:, :] = tail.astype(BF16)
    p = proj(3)
    rq_ref[...] = (p * sigmoid(p)).astype(BF16)
    for g, lbr, kr, lr in ((4, lbf_ref, kf_ref, lf_ref), (5, lbb_ref, kb_ref, lb_ref)):
        lbv = lbr[...]
        f = lbv + (1.0 - lbv) * sigmoid(proj(g))
        kr[...] = (1.0 - f).astype(BF16)
        lr[...] = jnp.log2(f)
    ri_ref[...] = proj(6).astype(BF16)
    p = proj(7)
    sg_ref[...] = (p * sigmoid(p)).astype(BF16)


def _inproj(x2, nw, w_in, seg, qw, kw, lbf, lbb, kaug, batch):
    n, d = x2.shape
    tm = TOKEN_TILE
    l = n // batch
    tps = l // tm
    aug_tiles = kaug.shape[0] // tm
    row = lambda i: (i, 0)
    out_bf = jax.ShapeDtypeStruct((n, GROUP_W), BF16)
    out_f = jax.ShapeDtypeStruct((n, GROUP_W), F32)
    ospec = pl.BlockSpec((tm, GROUP_W), row)
    kw_cols = kaug.shape[1]
    return pl.pallas_call(
        _inproj_body,
        grid=(n // tm,),
        in_specs=[pl.BlockSpec((tm, d), row), _const_spec((1, d)), _const_spec(w_in.shape),
                  _const_spec(seg.shape)] + [_const_spec((1, GROUP_W))] * 4
                 + [pl.BlockSpec((tm, kw_cols), lambda i: (i % aug_tiles, 0))],
        out_specs=[pl.BlockSpec((None, GROUP_W, tm), lambda i: (i // tps, 0, i % tps)),
                   pl.BlockSpec((tm, kw_cols), row),
                   pl.BlockSpec((None, ATTN_HEADS, ATTN_VROWS, tm),
                                lambda i: (i // tps, 0, 0, i % tps))] + [ospec] * 7,
        out_shape=[jax.ShapeDtypeStruct((batch, GROUP_W, l), BF16),
                   jax.ShapeDtypeStruct((n, kw_cols), BF16),
                   jax.ShapeDtypeStruct((batch, ATTN_HEADS, ATTN_VROWS, l), BF16),
                   out_bf, out_bf, out_bf, out_f, out_f, out_bf, out_bf],
        compiler_params=_cparams(("parallel",)),
    )(x2, nw, w_in, seg, qw, kw, lbf, lbb, kaug)


def _attn_body(c_ref, lam_ref, qt_ref, k_ref, vt_ref, aug_ref, wcol_ref, o_ref, m_ref, acc_ref,
               *, bq, bk, nkv, group_size):
    hd = pl.program_id(1)
    t0 = pl.program_id(2) * bq
    c = c_ref[hd]
    lam = lam_ref[0]
    jd = lax.div(t0, bk)
    tg = t0 + lax.broadcasted_iota(jnp.int32, (1, bq), 1)

    m_ref[...] = jnp.full(m_ref.shape, -jnp.inf, F32)
    acc_ref[...] = jnp.zeros(acc_ref.shape, F32)

    def scores(blk, mp):
        j, aug, _, bias = blk
        s0 = pl.multiple_of(j * bk, bk)
        qt = jnp.concatenate([qt_ref[mp * ATTN_QKDIM:(mp + 1) * ATTN_QKDIM, :], aug], axis=0)
        st = jnp.dot(k_ref[pl.ds(s0, bk), mp * 128:(mp + 1) * 128], qt,
                     preferred_element_type=F32)
        return st if bias is None else st + bias

    def fold(blk, mp, st):
        j, _, wcoef, _ = blk
        s0 = pl.multiple_of(j * bk, bk)
        w = wcoef * (tg - s0).astype(F32)
        m_old = m_ref[mp]
        m_new = jnp.maximum(m_old, jnp.max(st, axis=0, keepdims=True) + w)
        alpha = jnp.exp2(m_old - m_new)
        p = jnp.exp2((st - (m_new - w)).astype(BF16))
        pv = jnp.dot(vt_ref[:, pl.ds(s0, bk)], p, preferred_element_type=F32)
        acc_ref[mp] = acc_ref[mp] * alpha + pv
        m_ref[mp] = m_new

    def run(blocks):
        units = [(blk, mp) for blk in blocks for mp in range(2)]
        pending = [scores(*u) for u in units[:ATTN_LOOKAHEAD]]
        for n, unit in enumerate(units):
            if n + ATTN_LOOKAHEAD < len(units):
                pending.append(scores(*units[n + ATTN_LOOKAHEAD]))
            fold(*unit, pending.pop(0))

    rows = lax.broadcasted_iota(jnp.int32, (bk, bq), 0)
    dist = jnp.abs((tg - jd * bk) - rows).astype(F32)
    diag = (jd, jnp.zeros((ATTN_QKDIM, bq), BF16), 0.0, dist * (-c))

    def group_blocks(gi):
        blocks = []
        for u in range(group_size):
            idx = gi * group_size + u
            j = idx + (idx >= jd).astype(jnp.int32)
            sgn = jnp.where(j < jd, 1.0, -1.0)
            blocks.append((j, (aug_ref[...] * sgn).astype(BF16), -sgn * c, None))
        return blocks

    def group(gi, carry):
        run(group_blocks(gi))
        return carry

    run([diag] + group_blocks(0))
    lax.fori_loop(1, (nkv - 1) // group_size, group, 0)

    a1 = acc_ref[0]
    a2 = acc_ref[1]
    o1 = a1[:ATTN_VDIM] * (1.0 / a1[ATTN_VDIM:ATTN_VDIM + 1])
    o2 = a2[:ATTN_VDIM] * (1.0 / a2[ATTN_VDIM:ATTN_VDIM + 1])
    o = o1 - lam * o2
    ms = jnp.mean(o * o, axis=0, keepdims=True)
    y = o * lax.rsqrt(ms + NORM_EPS) * wcol_ref[...] * (1.0 - LAM_INIT)
    o_ref[...] = y.T.astype(BF16)


def _attn(cs, lam, qt, kp, vt, aug, wcol):
    b, h, _, l = vt.shape
    bq, bk = ATTN_BQ, ATTN_BK
    nkv = l // bk
    group_size = max(g for g in range(1, ATTN_MAX_GROUP + 1) if (nkv - 1) % g == 0)
    body = functools.partial(_attn_body, bq=bq, bk=bk, nkv=nkv, group_size=group_size)
    smem = pl.BlockSpec(memory_space=pltpu.SMEM)
    return pl.pallas_call(
        body,
        grid=(b, h, l // bq),
        in_specs=[smem, smem,
                  pl.BlockSpec((None, 2 * ATTN_QKDIM, bq), lambda bi, hi, i: (bi, hi, i)),
                  pl.BlockSpec((None, l, 256), lambda bi, hi, i: (bi, 0, hi)),
                  pl.BlockSpec((None, None, ATTN_VROWS, l), lambda bi, hi, i: (bi, hi, 0, 0)),
                  pl.BlockSpec((None, ATTN_QKDIM, bq), lambda bi, hi, i: (hi, 0, 0)),
                  _const_spec((ATTN_VDIM, bq))],
        out_specs=pl.BlockSpec((None, bq, ATTN_VDIM), lambda bi, hi, i: (bi, i, hi)),
        out_shape=jax.ShapeDtypeStruct((b, l, h * ATTN_VDIM), BF16),
        scratch_shapes=[pltpu.VMEM((2, 1, bq), F32), pltpu.VMEM((2, ATTN_VROWS, bq), F32)],
        compiler_params=_cparams(("parallel", "parallel", "arbitrary")),
    )(cs, lam, qt, kp, vt, aug, wcol)


def _gla_levels(c):
    lv, half = [], c // 2
    while half >= 8:
        lv.append(half)
        half //= 2
    return lv


def _gla_constants(c):
    r = np.arange(c)[:, None]
    u = np.arange(c)[None, :]
    mats = [(u <= r), (u > r)]
    bmasks = []
    for half in _gla_levels(c):
        blk = r // (2 * half)
        upper = (r % (2 * half)) >= half
        mid = blk * 2 * half + half - 1
        mats.append(np.where(upper, (u > mid) & (u <= r), (u > r) & (u <= mid)))
        t, s = r, u
        bmasks.append(((t // (2 * half)) == (s // (2 * half)))
                      & ((t % (2 * half)) >= half) & ((s % (2 * half)) < half))
    dmask = ((r // 8) == (u // 8)) & (u <= r)
    fwd_m = np.concatenate([m.astype(np.float32) for m in mats], axis=0)
    fwd_b = np.stack([m.astype(np.float32) for m in bmasks])
    fwd_d = dmask.astype(np.float32)
    flip = lambda m: m[::-1, ::-1]
    bwd_m = np.concatenate([flip(m.astype(np.float32)) for m in mats], axis=0)
    bwd_b = np.stack([flip(m.astype(np.float32)) for m in bmasks])
    bwd_d = flip(fwd_d)
    return (jnp.asarray(np.stack([fwd_m, bwd_m]), BF16),
            jnp.asarray(np.stack([fwd_b, bwd_b]), F32),
            jnp.asarray(np.stack([fwd_d, bwd_d]), F32))


def _gla_body(qf_ref, kf_ref, vf_ref, lf_ref, qb_ref, kb_ref, vb_ref, lb_ref,
              mst_ref, bm_ref, dm_ref, of_ref, ob_ref, s_ref, *, c, nch):
    nlev = len(_gla_levels(c))
    nt = (((1,), (1,)), ((), ()))

    @pl.when(pl.program_id(1) == 0)
    def _():
        s_ref[...] = jnp.zeros(s_ref.shape, F32)

    lane = lax.broadcasted_iota(jnp.int32, (8, c), 1)
    dirs = ((qf_ref, kf_ref, vf_ref, lf_ref, of_ref), (qb_ref, kb_ref, vb_ref, lb_ref, ob_ref))

    def chunk(ci, carry):
        exps, rows = [], []
        for d, (_, _, _, l_ref, _) in enumerate(dirs):
            cc = ci if d == 0 else nch - 1 - ci
            r0 = pl.multiple_of(cc * c, c)
            lg = l_ref[pl.ds(r0, c), :]
            h1 = lg.astype(BF16)
            h2 = (lg - h1.astype(F32)).astype(BF16)
            mst = mst_ref[d]
            exps.append(jnp.dot(mst, h1, preferred_element_type=F32)
                        + jnp.dot(mst, h2, preferred_element_type=F32))
            rows.append(r0)

        units = []
        for hd in range(REC_HEADS):
            cols = slice(hd * REC_DIM, (hd + 1) * REC_DIM)
            for d, (q_ref, k_ref, v_ref, _, _) in enumerate(dirs):
                ex, r0 = exps[d], rows[d]
                g = ex[0:c, cols]
                q = q_ref[pl.ds(r0, c), cols].astype(F32)
                k = k_ref[pl.ds(r0, c), cols].astype(F32)
                st = s_ref[d, hd]
                o = lax.dot_general((q * jnp.exp2(g)).astype(BF16), st.astype(BF16), nt,
                                    preferred_element_type=F32)
                a = jnp.zeros((c, c), F32)
                for li in range(nlev):
                    eh = jnp.exp2(ex[(2 + li) * c:(3 + li) * c, cols])
                    pr = lax.dot_general((q * eh).astype(BF16), (k * eh).astype(BF16), nt,
                                         preferred_element_type=F32)
                    a = a + bm_ref[d, li] * pr
                units.append((hd, d, cols, g, q, k, st, o, a))

        diags = []
        for hd, d, cols, g, q, k, st, o, a in units:
            strips = []
            for blk in range(c // 8):
                rs = slice(8 * blk, 8 * blk + 8)
                gb, qb, kb = g[rs], q[rs], k[rs]
                strip = jnp.zeros((8, c), F32)
                for s in range(8):
                    e = jnp.exp2(gb - gb[s:s + 1])
                    col = jnp.sum(qb * (kb[s:s + 1] * e), axis=-1, keepdims=True)
                    strip = jnp.where(lane == 8 * blk + s, col, strip)
                strips.append(strip)
            diags.append(jnp.where(dm_ref[d] > 0.0, jnp.concatenate(strips, axis=0), 0.0))

        for (hd, d, cols, g, q, k, st, o, a), dg in zip(units, diags):
            v_ref, o_ref = dirs[d][2], dirs[d][4]
            ex, r0 = exps[d], rows[d]
            v = v_ref[pl.ds(r0, c), cols]
            o = o + jnp.dot((a + dg).astype(BF16), v, preferred_element_type=F32)
            o_ref[pl.ds(r0, c), cols] = o
            kt = (k * jnp.exp2(ex[c:2 * c, cols])).astype(BF16)
            g_all = g[c - 1:c] if d == 0 else g[0:1]
            s_ref[d, hd] = st * jnp.exp2(g_all) + jnp.dot(
                v.astype(F32).T.astype(BF16), kt, preferred_element_type=F32)
        return carry

    lax.fori_loop(0, nch, chunk, 0)


def _gla(rq, kf, kb, ri, lf, lb, consts):
    b, l, w = rq.shape
    c, tc = GLA_CHUNK, GLA_TILE
    n = l // tc
    mst, bm, dm = consts
    fwd = pl.BlockSpec((None, tc, w), lambda bi, i: (bi, i, 0))
    bwd = pl.BlockSpec((None, tc, w), lambda bi, i: (bi, n - 1 - i, 0))
    out = jax.ShapeDtypeStruct((b, l, w), F32)
    return pl.pallas_call(
        functools.partial(_gla_body, c=c, nch=tc // c),
        grid=(b, n),
        in_specs=[fwd, fwd, fwd, fwd, bwd, bwd, bwd, bwd,
                  _const_spec(mst.shape), _const_spec(bm.shape), _const_spec(dm.shape)],
        out_specs=[fwd, bwd],
        out_shape=[out, out],
        scratch_shapes=[pltpu.VMEM((2, REC_HEADS, REC_DIM, REC_DIM), F32)],
        compiler_params=_cparams(("parallel", "arbitrary")),
    )(rq, kf, ri, lf, rq, kb, ri, lb, mst, bm, dm)


def _outproj_body(x_ref, ao_ref, of_ref, ob_ref, sg_ref, rw_ref, wo_ref, fw_ref, x1_ref, h2_ref):
    ro = of_ref[...] + ob_ref[...]
    parts = []
    for hd in range(REC_HEADS):
        seg = ro[:, hd * REC_DIM:(hd + 1) * REC_DIM]
        ms = jnp.mean(seg * seg, axis=-1, keepdims=True)
        parts.append(seg * lax.rsqrt(ms + NORM_EPS))
    ron = jnp.concatenate(parts, axis=-1) * rw_ref[...] * sg_ref[...].astype(F32)
    aw = ao_ref.shape[-1]
    mix = (jnp.dot(ao_ref[...], wo_ref[:aw, :], preferred_element_type=F32)
           + jnp.dot(ron.astype(BF16), wo_ref[aw:, :], preferred_element_type=F32))
    x1 = x_ref[...] + mix
    x1_ref[...] = x1
    ms = jnp.mean(x1 * x1, axis=-1, keepdims=True)
    h2_ref[...] = (x1 * lax.rsqrt(ms + NORM_EPS) * fw_ref[...]).astype(BF16)


def _outproj(x2, ao, o_f, o_b, sg, rw, w_out, fw):
    n, d = x2.shape
    tm = TOKEN_TILE
    row = lambda i: (i, 0)
    half = pl.BlockSpec((tm, GROUP_W), row)
    full = pl.BlockSpec((tm, d), row)
    return pl.pallas_call(
        _outproj_body,
        grid=(n // tm,),
        in_specs=[full, half, half, half, half, _const_spec((1, GROUP_W)),
                  _const_spec(w_out.shape), _const_spec((1, d))],
        out_specs=[full, full],
        out_shape=[jax.ShapeDtypeStruct((n, d), F32), jax.ShapeDtypeStruct((n, d), BF16)],
        compiler_params=_cparams(("parallel",)),
    )(x2, ao, o_f, o_b, sg, rw, w_out, fw)


def _ffn_body(hp_ref, hc_ref, hn_ref, x1_ref, wu_ref, cw_ref, cb_ref, wd_ref, y_ref, acc_ref,
              u_ref,
              *, tm, ft, dff, tiles_per_seq):
    i = pl.program_id(0)
    pos = lax.rem(i, tiles_per_seq)
    hp = jnp.where(pos == 0, jnp.zeros_like(hp_ref[...]), hp_ref[...])
    hn = jnp.where(pos == tiles_per_seq - 1, jnp.zeros_like(hn_ref[...]), hn_ref[...])
    hx = jnp.concatenate([hp, hc_ref[...], hn], axis=0)
    ext = tm + 2 * HALO
    acc_ref[...] = x1_ref[...]

    def conv(u, col):
        prev = pltpu.roll(u, 1, axis=0)[HALO:HALO + tm]
        nxt = pltpu.roll(u, ext - 1, axis=0)[HALO:HALO + tm]
        cw = cw_ref[:, pl.ds(col, ft)]
        return (cb_ref[:, pl.ds(col, ft)] + prev * cw[0:1] + u[HALO:HALO + tm] * cw[1:2]
                + nxt * cw[2:3])

    def up_dots(f, slot):
        cg = pl.multiple_of(f * ft, ft)
        cu = pl.multiple_of(dff + f * ft, ft)
        u_ref[slot, 0] = jnp.dot(hx, wu_ref[:, pl.ds(cg, ft)], preferred_element_type=F32)
        u_ref[slot, 1] = jnp.dot(hx, wu_ref[:, pl.ds(cu, ft)], preferred_element_type=F32)

    def finish(f, slot):
        cg = pl.multiple_of(f * ft, ft)
        cu = pl.multiple_of(dff + f * ft, ft)
        gate = conv(u_ref[slot, 0], cg)
        up = conv(u_ref[slot, 1], cu)
        act = (gate * (1.0 / (1.0 + jnp.exp(-gate))) * up).astype(BF16)
        acc_ref[...] += jnp.dot(act, wd_ref[pl.ds(cg, ft), :], preferred_element_type=F32)

    def pair(i, carry):
        f = 2 * i
        up_dots(f + 1, 1)
        finish(f, 0)
        up_dots(f + 2, 0)
        finish(f + 1, 1)
        return carry

    nf = dff // ft
    up_dots(0, 0)
    lax.fori_loop(0, (nf - 1) // 2, pair, 0)
    for f in range(2 * ((nf - 1) // 2), nf):
        if f + 1 < nf:
            up_dots(f + 1, (f + 1) % 2)
        finish(f, f % 2)
    y_ref[...] = acc_ref[...]


def _ffn(h2, x1, w_up, conv_w, conv_b, w_down, seq_len):
    n, d = x1.shape
    tm, ft = TOKEN_TILE, FFN_FT
    dff = w_down.shape[0]
    nt = n // tm
    hb = tm // HALO
    nhb = n // HALO
    row = lambda i: (i, 0)
    body = functools.partial(_ffn_body, tm=tm, ft=ft, dff=dff, tiles_per_seq=seq_len // tm)
    return pl.pallas_call(
        body,
        grid=(nt,),
        in_specs=[pl.BlockSpec((HALO, d), lambda i: (jnp.maximum(i * hb - 1, 0), 0)),
                  pl.BlockSpec((tm, d), row),
                  pl.BlockSpec((HALO, d), lambda i: (jnp.minimum((i + 1) * hb, nhb - 1), 0)),
                  pl.BlockSpec((tm, d), row),
                  _const_spec(w_up.shape), _const_spec(conv_w.shape), _const_spec(conv_b.shape),
                  _const_spec(w_down.shape)],
        out_specs=pl.BlockSpec((tm, d), row),
        out_shape=jax.ShapeDtypeStruct((n, d), F32),
        scratch_shapes=[pltpu.VMEM((tm, d), F32), pltpu.VMEM((2, 2, tm + 2 * HALO, ft), F32)],
        compiler_params=_cparams(("parallel",)),
    )(h2, h2, h2, x1, w_up, conv_w, conv_b, w_down)


def _bf16_split3(x):
    x = x.astype(F32)
    a = x.astype(BF16)
    r = x - a.astype(F32)
    b = r.astype(BF16)
    c = (r - b.astype(F32)).astype(BF16)
    return a, b, c


def _alibi_operands():
    start = 2.0 ** (-8.0 / ATTN_HEADS)
    slopes = np.array([start ** (i + 1) for i in range(ATTN_HEADS)], np.float32)
    cs = jnp.asarray(slopes * np.float32(LOG2E), F32)
    groups = GROUP_W // ATTN_QKDIM
    pos = np.arange(ATTN_BK)
    kaug = np.zeros((ATTN_BK, groups, 128), np.float32)
    kaug[:, :, ATTN_QKDIM:ATTN_QKDIM + 3] = (pos - pos % 16)[:, None, None]
    kaug[:, :, ATTN_QKDIM + 3:ATTN_QKDIM + 6] = (pos % 16)[:, None, None]
    c1, c2, c3 = _bf16_split3(cs)
    six = jnp.stack([c1, c2, c3, c1, c2, c3], axis=1).astype(F32)
    rows = jnp.zeros((ATTN_HEADS, ATTN_QKDIM), F32).at[:, :6].set(six)
    aug = jnp.broadcast_to(rows[..., None], rows.shape + (ATTN_BQ,))
    return cs, jnp.asarray(kaug.reshape(ATTN_BK, groups * 128), F32), aug


def _trunk(x, norm_mix_w, w_in, q_norm_w, k_norm_w, lam, attn_out_norm_w, lb_fwd, lb_bwd,
           rec_out_norm_w, w_out, norm_ffn_w, w_up, conv_w, conv_b, w_down):
    b, l, d = x.shape
    n = b * l
    x2 = x.reshape(n, d)

    seg = np.kron(np.eye(GROUP_W // ATTN_QKDIM, dtype=np.float32),
                  np.full((ATTN_QKDIM, ATTN_QKDIM), 1.0 / ATTN_QKDIM, np.float32))
    reps = GROUP_W // ATTN_QKDIM
    lbf = jnp.cumsum(jax.nn.softmax(lb_fwd.astype(F32), axis=0), axis=0)[0][None]
    lbb = jnp.cumsum(jax.nn.softmax(lb_bwd.astype(F32), axis=0), axis=0)[0][None]
    cs, kaug, aug = _alibi_operands()
    qt, kp, vt, rq, kf, kb, lf, lb, ri, sg = _inproj(
        x2, norm_mix_w[None], w_in.astype(BF16), jnp.asarray(seg, BF16),
        jnp.tile(q_norm_w, reps)[None], jnp.tile(k_norm_w, reps)[None], lbf, lbb, kaug, b)

    wcol = jnp.broadcast_to(attn_out_norm_w.astype(F32)[:, None], (ATTN_VDIM, ATTN_BQ))
    ao = _attn(cs, lam.reshape(1), qt, kp.reshape(b, l, kp.shape[-1]), vt, aug, wcol)

    shape3 = (b, l, GROUP_W)
    o_f, o_b = _gla(rq.reshape(shape3), kf.reshape(shape3), kb.reshape(shape3),
                    ri.reshape(shape3), lf.reshape(shape3), lb.reshape(shape3),
                    _gla_constants(GLA_CHUNK))

    x1, h2 = _outproj(x2, ao.reshape(n, GROUP_W), o_f.reshape(n, GROUP_W),
                      o_b.reshape(n, GROUP_W), sg, jnp.tile(rec_out_norm_w, REC_HEADS)[None],
                      w_out.astype(BF16), norm_ffn_w[None])
    y = _ffn(h2, x1, w_up.astype(BF16), conv_w, conv_b[None], w_down.astype(BF16), l)
    return y.reshape(b, l, d)


def kernel(x_prompt, x_sample, norm_mix_w, w_in, q_norm_w, k_norm_w, lambda_q1, lambda_k1,
           lambda_q2, lambda_k2, attn_out_norm_w, lb_fwd, lb_bwd, rec_out_norm_w, w_out,
           norm_ffn_w, w_up, conv_w, conv_b, w_down):
    assert norm_mix_w.shape[0] == 1, "single-layer trunk"
    lam = (jnp.exp(jnp.sum(lambda_q1[0].astype(F32) * lambda_k1[0].astype(F32)))
           - jnp.exp(jnp.sum(lambda_q2[0].astype(F32) * lambda_k2[0].astype(F32))) + LAM_INIT)
    params = (norm_mix_w[0], w_in[0], q_norm_w[0], k_norm_w[0], lam, attn_out_norm_w[0],
              lb_fwd, lb_bwd, rec_out_norm_w[0], w_out[0], norm_ffn_w[0], w_up[0], conv_w[0],
              conv_b[0], w_down[0])
    return (_trunk(x_prompt, *params), _trunk(x_sample, *params))
```

```python
import functools
import math

import numpy as np
import jax
import jax.numpy as jnp
from jax import lax
from jax.experimental import pallas as pl
from jax.experimental.pallas import tpu as pltpu

F32 = jnp.float32
BF16 = jnp.bfloat16

NORM_EPS = 1e-6
ATTN_HEADS = 4
ATTN_QKDIM = 64
ATTN_VDIM = 128
REC_HEADS = 4
REC_DIM = 128
GROUP_W = 512
LOG2E = 1.4426950408889634
LAM_INIT = 0.8 - 0.6 * math.exp(-0.3 * 0)

V7X_VMEM_LIMIT = 56 * 1024 * 1024

TOKEN_TILE = 512
ATTN_BQ = 256
ATTN_BK = 256
ATTN_MAX_GROUP = 31
ATTN_LOOKAHEAD = 4
ATTN_FIXED_SHIFT_RANGE = 64.0
ATTN_VROWS = ATTN_VDIM + 16
GLA_CHUNK = 128
GLA_TILE = 512
FFN_FT = 256
HALO = 16


def _cparams(sem):
    return pltpu.CompilerParams(dimension_semantics=sem, vmem_limit_bytes=V7X_VMEM_LIMIT)


def _const_spec(shape):
    zeros = (0,) * len(shape)
    return pl.BlockSpec(shape, lambda *_: zeros)


def _inproj_body(x_ref, nw_ref, w_ref, seg_ref, qw_ref, kw_ref, lbf_ref, lbb_ref,
                 kaug_ref, q_ref, k_ref, v_ref, rq_ref, kf_ref, kb_ref, lf_ref, lb_ref, ri_ref,
                 sg_ref):
    x = x_ref[...]
    ms = jnp.mean(x * x, axis=-1, keepdims=True)
    h = (x * lax.rsqrt(ms + NORM_EPS) * nw_ref[...]).astype(BF16)

    def proj(g):
        return jnp.dot(h, w_ref[:, g * GROUP_W:(g + 1) * GROUP_W], preferred_element_type=F32)

    def head_norm(p, w):
        m = jnp.dot((p * p).astype(BF16), seg_ref[...], preferred_element_type=F32)
        return p * lax.rsqrt(m + NORM_EPS) * w

    def sigmoid(p):
        return 1.0 / (1.0 + jnp.exp(-p))

    qn = head_norm(proj(0), qw_ref[...]) * (ATTN_QKDIM ** -0.5 * LOG2E)
    q_ref[...] = qn.T.astype(BF16)
    kn = head_norm(proj(1), kw_ref[...])
    low = lax.broadcasted_iota(jnp.int32, (kn.shape[0], 128), 1) < ATTN_QKDIM
    for t in range(GROUP_W // 128):
        pair = kn[:, t * 128:(t + 1) * 128]
        for half, src in ((0, pair), (1, pltpu.roll(pair, ATTN_QKDIM, axis=1))):
            cols = slice((2 * t + half) * 128, (2 * t + half + 1) * 128)
            k_ref[:, cols] = jnp.where(low, src, kaug_ref[:, cols]).astype(BF16)
    vt = proj(2).T
    tm = vt.shape[-1]
    tail = (lax.broadcasted_iota(jnp.int32, (ATTN_VROWS - ATTN_VDIM, tm), 0) == 0)
    for hd in range(ATTN_HEADS):
        v_ref[hd, :ATTN_VDIM, :] = vt[hd * ATTN_VDIM:(hd + 1) * ATTN_VDIM].astype(BF16)
        v_ref[hd, ATTN_VDIM:, :] = tail.astype(BF16)
    p = proj(3)
    rq_ref[...] = (p * sigmoid(p)).astype(BF16)
    for g, lbr, kr, lr in ((4, lbf_ref, kf_ref, lf_ref), (5, lbb_ref, kb_ref, lb_ref)):
        lbv = lbr[...]
        f = lbv + (1.0 - lbv) * sigmoid(proj(g))
        kr[...] = (1.0 - f).astype(BF16)
        lr[...] = jnp.log2(f)
    ri_ref[...] = proj(6).astype(BF16)
    p = proj(7)
    sg_ref[...] = (p * sigmoid(p)).astype(BF16)


def _inproj(x2, nw, w_in, seg, qw, kw, lbf, lbb, kaug, batch):
    n, d = x2.shape
    tm = TOKEN_TILE
    l = n // batch
    tps = l // tm
    aug_tiles = kaug.shape[0] // tm
    row = lambda i: (i, 0)
    out_bf = jax.ShapeDtypeStruct((n, GROUP_W), BF16)
    out_f = jax.ShapeDtypeStruct((n, GROUP_W), F32)
    ospec = pl.BlockSpec((tm, GROUP_W), row)
    kw_cols = kaug.shape[1]
    return pl.pallas_call(
        _inproj_body,
        grid=(n // tm,),
        in_specs=[pl.BlockSpec((tm, d), row), _const_spec((1, d)), _const_spec(w_in.shape),
                  _const_spec(seg.shape)] + [_const_spec((1, GROUP_W))] * 4
                 + [pl.BlockSpec((tm, kw_cols), lambda i: (i % aug_tiles, 0))],
        out_specs=[pl.BlockSpec((None, GROUP_W, tm), lambda i: (i // tps, 0, i % tps)),
                   pl.BlockSpec((tm, kw_cols), row),
                   pl.BlockSpec((None, ATTN_HEADS, ATTN_VROWS, tm),
                                lambda i: (i // tps, 0, 0, i % tps))] + [ospec] * 7,
        out_shape=[jax.ShapeDtypeStruct((batch, GROUP_W, l), BF16),
                   jax.ShapeDtypeStruct((n, kw_cols), BF16),
                   jax.ShapeDtypeStruct((batch, ATTN_HEADS, ATTN_VROWS, l), BF16),
                   out_bf, out_bf, out_bf, out_f, out_f, out_bf, out_bf],
        compiler_params=_cparams(("parallel",)),
    )(x2, nw, w_in, seg, qw, kw, lbf, lbb, kaug)


def _attn_body(c_ref, lam_ref, qt_ref, k_ref, vt_ref, aug_ref, wcol_ref, o_ref, m_ref, acc_ref,
               *, bq, bk, nkv, group_size, online):
    hd = pl.program_id(1)
    t0 = pl.program_id(2) * bq
    c = c_ref[hd]
    lam = lam_ref[0]
    jd = lax.div(t0, bk)
    tg = t0 + lax.broadcasted_iota(jnp.int32, (1, bq), 1)

    if online:
        m_ref[...] = jnp.full(m_ref.shape, -jnp.inf, F32)
        acc_ref[...] = jnp.zeros(acc_ref.shape, F32)

    def scores(blk, mp):
        j, aug, _, bias = blk
        s0 = pl.multiple_of(j * bk, bk)
        qt = jnp.concatenate([qt_ref[mp * ATTN_QKDIM:(mp + 1) * ATTN_QKDIM, :], aug], axis=0)
        st = jnp.dot(k_ref[pl.ds(s0, bk), mp * 128:(mp + 1) * 128], qt,
                     preferred_element_type=F32)
        return st if bias is None else st + bias

    def fold(blk, mp, st):
        j, _, wcoef, bias = blk
        s0 = pl.multiple_of(j * bk, bk)
        w = wcoef * (tg - s0).astype(F32)
        vt = vt_ref[:, pl.ds(s0, bk)]
        if online:
            m_old = m_ref[mp]
            m_new = jnp.maximum(m_old, jnp.max(st, axis=0, keepdims=True) + w)
            alpha = jnp.exp2(m_old - m_new)
            p = jnp.exp2((st - (m_new - w)).astype(BF16))
            acc_ref[mp] = acc_ref[mp] * alpha + jnp.dot(vt, p, preferred_element_type=F32)
            m_ref[mp] = m_new
        elif bias is not None:
            shift = jnp.max(st, axis=0, keepdims=True)
            m_ref[mp] = shift
            p = jnp.exp2((st - shift).astype(BF16))
            acc_ref[mp] = jnp.dot(vt, p, preferred_element_type=F32)
        else:
            p = jnp.exp2((st - (m_ref[mp] - w)).astype(BF16))
            acc_ref[mp] += jnp.dot(vt, p, preferred_element_type=F32)

    def run(blocks):
        units = [(blk, mp) for blk in blocks for mp in range(2)]
        pending = [scores(*u) for u in units[:ATTN_LOOKAHEAD]]
        for n, unit in enumerate(units):
            if n + ATTN_LOOKAHEAD < len(units):
                pending.append(scores(*units[n + ATTN_LOOKAHEAD]))
            fold(*unit, pending.pop(0))

    rows = lax.broadcasted_iota(jnp.int32, (bk, bq), 0)
    dist = jnp.abs((tg - jd * bk) - rows).astype(F32)
    diag = (jd, jnp.zeros((ATTN_QKDIM, bq), BF16), 0.0, dist * (-c))

    def off_blocks(first, count):
        blocks = []
        for u in range(count):
            idx = first + u
            j = idx + (idx >= jd).astype(jnp.int32)
            sgn = jnp.where(j < jd, 1.0, -1.0)
            blocks.append((j, (aug_ref[...] * sgn).astype(BF16), -sgn * c, None))
        return blocks

    n_off = nkv - 1
    head = group_size + n_off % group_size

    def group(gi, carry):
        run(off_blocks(head + gi * group_size, group_size))
        return carry

    run([diag] + off_blocks(0, head))
    lax.fori_loop(0, n_off // group_size - 1, group, 0)

    a1 = acc_ref[0]
    a2 = acc_ref[1]
    o1 = a1[:ATTN_VDIM] * (1.0 / a1[ATTN_VDIM:ATTN_VDIM + 1])
    o2 = a2[:ATTN_VDIM] * (1.0 / a2[ATTN_VDIM:ATTN_VDIM + 1])
    o = o1 - lam * o2
    ms = jnp.mean(o * o, axis=0, keepdims=True)
    y = o * lax.rsqrt(ms + NORM_EPS) * wcol_ref[...] * (1.0 - LAM_INIT)
    o_ref[...] = y.T.astype(BF16)


def _attn(cs, lam, qt, kp, vt, aug, wcol, online):
    b, h, _, l = vt.shape
    bq, bk = ATTN_BQ, ATTN_BK
    nkv = l // bk
    assert nkv >= 2
    group_size = min(ATTN_MAX_GROUP, nkv - 1)
    body = functools.partial(_attn_body, bq=bq, bk=bk, nkv=nkv, group_size=group_size,
                             online=online)
    smem = pl.BlockSpec(memory_space=pltpu.SMEM)
    return pl.pallas_call(
        body,
        grid=(b, h, l // bq),
        in_specs=[smem, smem,
                  pl.BlockSpec((None, 2 * ATTN_QKDIM, bq), lambda bi, hi, i: (bi, hi, i)),
                  pl.BlockSpec((None, l, 256), lambda bi, hi, i: (bi, 0, hi)),
                  pl.BlockSpec((None, None, ATTN_VROWS, l), lambda bi, hi, i: (bi, hi, 0, 0)),
                  pl.BlockSpec((None, ATTN_QKDIM, bq), lambda bi, hi, i: (hi, 0, 0)),
                  _const_spec((ATTN_VDIM, bq))],
        out_specs=pl.BlockSpec((None, bq, ATTN_VDIM), lambda bi, hi, i: (bi, i, hi)),
        out_shape=jax.ShapeDtypeStruct((b, l, h * ATTN_VDIM), BF16),
        scratch_shapes=[pltpu.VMEM((2, 1, bq), F32), pltpu.VMEM((2, ATTN_VROWS, bq), F32)],
        compiler_params=_cparams(("parallel", "parallel", "arbitrary")),
    )(cs, lam, qt, kp, vt, aug, wcol)


def _gla_levels(c):
    lv, half = [], c // 2
    while half >= 8:
        lv.append(half)
        half //= 2
    return lv


def _gla_constants(c):
    r = np.arange(c)[:, None]
    u = np.arange(c)[None, :]
    mats = [(u <= r), (u > r)]
    bmasks = []
    for half in _gla_levels(c):
        blk = r // (2 * half)
        upper = (r % (2 * half)) >= half
        mid = blk * 2 * half + half - 1
        mats.append(np.where(upper, (u > mid) & (u <= r), (u > r) & (u <= mid)))
        t, s = r, u
        bmasks.append(((t // (2 * half)) == (s // (2 * half)))
                      & ((t % (2 * half)) >= half) & ((s % (2 * half)) < half))
    dmask = ((r // 8) == (u // 8)) & (u <= r)
    fwd_m = np.concatenate([m.astype(np.float32) for m in mats], axis=0)
    fwd_b = np.stack([m.astype(np.float32) for m in bmasks])
    fwd_d = dmask.astype(np.float32)
    flip = lambda m: m[::-1, ::-1]
    bwd_m = np.concatenate([flip(m.astype(np.float32)) for m in mats], axis=0)
    bwd_b = np.stack([flip(m.astype(np.float32)) for m in bmasks])
    bwd_d = flip(fwd_d)
    return (jnp.asarray(np.stack([fwd_m, bwd_m]), BF16),
            jnp.asarray(np.stack([fwd_b, bwd_b]), F32),
            jnp.asarray(np.stack([fwd_d, bwd_d]), F32))


def _gla_body(qf_ref, kf_ref, vf_ref, lf_ref, qb_ref, kb_ref, vb_ref, lb_ref,
              mst_ref, bm_ref, dm_ref, of_ref, ob_ref, s_ref, *, c, nch):
    nlev = len(_gla_levels(c))
    nt = (((1,), (1,)), ((), ()))

    @pl.when(pl.program_id(1) == 0)
    def _():
        s_ref[...] = jnp.zeros(s_ref.shape, F32)

    lane = lax.broadcasted_iota(jnp.int32, (8, c), 1)
    dirs = ((qf_ref, kf_ref, vf_ref, lf_ref, of_ref), (qb_ref, kb_ref, vb_ref, lb_ref, ob_ref))

    def chunk(ci, carry):
        exps, rows = [], []
        for d, (_, _, _, l_ref, _) in enumerate(dirs):
            cc = ci if d == 0 else nch - 1 - ci
            r0 = pl.multiple_of(cc * c, c)
            lg = l_ref[pl.ds(r0, c), :]
            h1 = lg.astype(BF16)
            h2 = (lg - h1.astype(F32)).astype(BF16)
            mst = mst_ref[d]
            exps.append(jnp.dot(mst, h1, preferred_element_type=F32)
                        + jnp.dot(mst, h2, preferred_element_type=F32))
            rows.append(r0)

        units = []
        for hd in range(REC_HEADS):
            cols = slice(hd * REC_DIM, (hd + 1) * REC_DIM)
            for d, (q_ref, k_ref, v_ref, _, _) in enumerate(dirs):
                ex, r0 = exps[d], rows[d]
                g = ex[0:c, cols]
                q = q_ref[pl.ds(r0, c), cols].astype(F32)
                k = k_ref[pl.ds(r0, c), cols].astype(F32)
                st = s_ref[d, hd]
                o = lax.dot_general((q * jnp.exp2(g)).astype(BF16), st.astype(BF16), nt,
                                    preferred_element_type=F32)
                a = jnp.zeros((c, c), F32)
                for li in range(nlev):
                    eh = jnp.exp2(ex[(2 + li) * c:(3 + li) * c, cols])
                    pr = lax.dot_general((q * eh).astype(BF16), (k * eh).astype(BF16), nt,
                                         preferred_element_type=F32)
                    a = a + bm_ref[d, li] * pr
                units.append((hd, d, cols, g, q, k, st, o, a))

        diags = []
        for hd, d, cols, g, q, k, st, o, a in units:
            strips = []
            for blk in range(c // 8):
                rs = slice(8 * blk, 8 * blk + 8)
                gb, qb, kb = g[rs], q[rs], k[rs]
                strip = jnp.zeros((8, c), F32)
                for s in range(8):
                    e = jnp.exp2(gb - gb[s:s + 1])
                    col = jnp.sum(qb * (kb[s:s + 1] * e), axis=-1, keepdims=True)
                    strip = jnp.where(lane == 8 * blk + s, col, strip)
                strips.append(strip)
            diags.append(jnp.where(dm_ref[d] > 0.0, jnp.concatenate(strips, axis=0), 0.0))

        for (hd, d, cols, g, q, k, st, o, a), dg in zip(units, diags):
            v_ref, o_ref = dirs[d][2], dirs[d][4]
            ex, r0 = exps[d], rows[d]
            v = v_ref[pl.ds(r0, c), cols]
            o = o + jnp.dot((a + dg).astype(BF16), v, preferred_element_type=F32)
            o_ref[pl.ds(r0, c), cols] = o
            kt = (k * jnp.exp2(ex[c:2 * c, cols])).astype(BF16)
            g_all = g[c - 1:c] if d == 0 else g[0:1]
            s_ref[d, hd] = st * jnp.exp2(g_all) + jnp.dot(
                v.astype(F32).T.astype(BF16), kt, preferred_element_type=F32)
        return carry

    lax.fori_loop(0, nch, chunk, 0)


def _gla(rq, kf, kb, ri, lf, lb, consts):
    b, l, w = rq.shape
    c, tc = GLA_CHUNK, GLA_TILE
    n = l // tc
    mst, bm, dm = consts
    fwd = pl.BlockSpec((None, tc, w), lambda bi, i: (bi, i, 0))
    bwd = pl.BlockSpec((None, tc, w), lambda bi, i: (bi, n - 1 - i, 0))
    out = jax.ShapeDtypeStruct((b, l, w), F32)
    return pl.pallas_call(
        functools.partial(_gla_body, c=c, nch=tc // c),
        grid=(b, n),
        in_specs=[fwd, fwd, fwd, fwd, bwd, bwd, bwd, bwd,
                  _const_spec(mst.shape), _const_spec(bm.shape), _const_spec(dm.shape)],
        out_specs=[fwd, bwd],
        out_shape=[out, out],
        scratch_shapes=[pltpu.VMEM((2, REC_HEADS, REC_DIM, REC_DIM), F32)],
        compiler_params=_cparams(("parallel", "arbitrary")),
    )(rq, kf, ri, lf, rq, kb, ri, lb, mst, bm, dm)


def _outproj_body(x_ref, ao_ref, of_ref, ob_ref, sg_ref, rw_ref, wo_ref, fw_ref, x1_ref, h2_ref):
    ro = of_ref[...] + ob_ref[...]
    parts = []
    for hd in range(REC_HEADS):
        seg = ro[:, hd * REC_DIM:(hd + 1) * REC_DIM]
        ms = jnp.mean(seg * seg, axis=-1, keepdims=True)
        parts.append(seg * lax.rsqrt(ms + NORM_EPS))
    ron = jnp.concatenate(parts, axis=-1) * rw_ref[...] * sg_ref[...].astype(F32)
    aw = ao_ref.shape[-1]
    mix = (jnp.dot(ao_ref[...], wo_ref[:aw, :], preferred_element_type=F32)
           + jnp.dot(ron.astype(BF16), wo_ref[aw:, :], preferred_element_type=F32))
    x1 = x_ref[...] + mix
    x1_ref[...] = x1
    ms = jnp.mean(x1 * x1, axis=-1, keepdims=True)
    h2_ref[...] = (x1 * lax.rsqrt(ms + NORM_EPS) * fw_ref[...]).astype(BF16)


def _outproj(x2, ao, o_f, o_b, sg, rw, w_out, fw):
    n, d = x2.shape
    tm = TOKEN_TILE
    row = lambda i: (i, 0)
    half = pl.BlockSpec((tm, GROUP_W), row)
    full = pl.BlockSpec((tm, d), row)
    return pl.pallas_call(
        _outproj_body,
        grid=(n // tm,),
        in_specs=[full, half, half, half, half, _const_spec((1, GROUP_W)),
                  _const_spec(w_out.shape), _const_spec((1, d))],
        out_specs=[full, full],
        out_shape=[jax.ShapeDtypeStruct((n, d), F32), jax.ShapeDtypeStruct((n, d), BF16)],
        compiler_params=_cparams(("parallel",)),
    )(x2, ao, o_f, o_b, sg, rw, w_out, fw)


def _ffn_body(hp_ref, hc_ref, hn_ref, x1_ref, wu_ref, cw_ref, cb_ref, wd_ref, y_ref, acc_ref,
              u_ref,
              *, tm, ft, dff, tiles_per_seq):
    i = pl.program_id(0)
    pos = lax.rem(i, tiles_per_seq)
    hp = jnp.where(pos == 0, jnp.zeros_like(hp_ref[...]), hp_ref[...])
    hn = jnp.where(pos == tiles_per_seq - 1, jnp.zeros_like(hn_ref[...]), hn_ref[...])
    hx = jnp.concatenate([hp, hc_ref[...], hn], axis=0)
    ext = tm + 2 * HALO
    acc_ref[...] = x1_ref[...]

    def conv(u, col):
        prev = pltpu.roll(u, 1, axis=0)[HALO:HALO + tm]
        nxt = pltpu.roll(u, ext - 1, axis=0)[HALO:HALO + tm]
        cw = cw_ref[:, pl.ds(col, ft)]
        return (cb_ref[:, pl.ds(col, ft)] + prev * cw[0:1] + u[HALO:HALO + tm] * cw[1:2]
                + nxt * cw[2:3])

    def up_dots(f, slot):
        cg = pl.multiple_of(f * ft, ft)
        cu = pl.multiple_of(dff + f * ft, ft)
        u_ref[slot, 0] = jnp.dot(hx, wu_ref[:, pl.ds(cg, ft)], preferred_element_type=F32)
        u_ref[slot, 1] = jnp.dot(hx, wu_ref[:, pl.ds(cu, ft)], preferred_element_type=F32)

    def finish(f, slot):
        cg = pl.multiple_of(f * ft, ft)
        cu = pl.multiple_of(dff + f * ft, ft)
        gate = conv(u_ref[slot, 0], cg)
        up = conv(u_ref[slot, 1], cu)
        act = (gate * (1.0 / (1.0 + jnp.exp(-gate))) * up).astype(BF16)
        acc_ref[...] += jnp.dot(act, wd_ref[pl.ds(cg, ft), :], preferred_element_type=F32)

    def pair(i, carry):
        f = 2 * i
        up_dots(f + 1, 1)
        finish(f, 0)
        up_dots(f + 2, 0)
        finish(f + 1, 1)
        return carry

    nf = dff // ft
    up_dots(0, 0)
    lax.fori_loop(0, (nf - 1) // 2, pair, 0)
    for f in range(2 * ((nf - 1) // 2), nf):
        if f + 1 < nf:
            up_dots(f + 1, (f + 1) % 2)
        finish(f, f % 2)
    y_ref[...] = acc_ref[...]


def _ffn(h2, x1, w_up, conv_w, conv_b, w_down, seq_len):
    n, d = x1.shape
    tm, ft = TOKEN_TILE, FFN_FT
    dff = w_down.shape[0]
    nt = n // tm
    hb = tm // HALO
    nhb = n // HALO
    row = lambda i: (i, 0)
    body = functools.partial(_ffn_body, tm=tm, ft=ft, dff=dff, tiles_per_seq=seq_len // tm)
    return pl.pallas_call(
        body,
        grid=(nt,),
        in_specs=[pl.BlockSpec((HALO, d), lambda i: (jnp.maximum(i * hb - 1, 0), 0)),
                  pl.BlockSpec((tm, d), row),
                  pl.BlockSpec((HALO, d), lambda i: (jnp.minimum((i + 1) * hb, nhb - 1), 0)),
                  pl.BlockSpec((tm, d), row),
                  _const_spec(w_up.shape), _const_spec(conv_w.shape), _const_spec(conv_b.shape),
                  _const_spec(w_down.shape)],
        out_specs=pl.BlockSpec((tm, d), row),
        out_shape=jax.ShapeDtypeStruct((n, d), F32),
        scratch_shapes=[pltpu.VMEM((tm, d), F32), pltpu.VMEM((2, 2, tm + 2 * HALO, ft), F32)],
        compiler_params=_cparams(("parallel",)),
    )(h2, h2, h2, x1, w_up, conv_w, conv_b, w_down)


def _bf16_split3(x):
    x = x.astype(F32)
    a = x.astype(BF16)
    r = x - a.astype(F32)
    b = r.astype(BF16)
    c = (r - b.astype(F32)).astype(BF16)
    return a, b, c


def _alibi_operands():
    start = 2.0 ** (-8.0 / ATTN_HEADS)
    slopes = np.array([start ** (i + 1) for i in range(ATTN_HEADS)], np.float32)
    cs = jnp.asarray(slopes * np.float32(LOG2E), F32)
    groups = GROUP_W // ATTN_QKDIM
    nrows = max(ATTN_BK, TOKEN_TILE)
    pos = np.arange(nrows) % ATTN_BK
    kaug = np.zeros((nrows, groups, 128), np.float32)
    kaug[:, :, ATTN_QKDIM:ATTN_QKDIM + 3] = (pos - pos % 16)[:, None, None]
    kaug[:, :, ATTN_QKDIM + 3:ATTN_QKDIM + 6] = (pos % 16)[:, None, None]
    c1, c2, c3 = _bf16_split3(cs)
    six = jnp.stack([c1, c2, c3, c1, c2, c3], axis=1).astype(F32)
    rows = jnp.zeros((ATTN_HEADS, ATTN_QKDIM), F32).at[:, :6].set(six)
    aug = jnp.broadcast_to(rows[..., None], rows.shape + (ATTN_BQ,))
    return cs, jnp.asarray(kaug.reshape(nrows, groups * 128), F32), aug


def _trunk(x, norm_mix_w, w_in, q_norm_w, k_norm_w, lam, attn_out_norm_w, lb_fwd, lb_bwd,
           rec_out_norm_w, w_out, norm_ffn_w, w_up, conv_w, conv_b, w_down):
    b, l, d = x.shape
    n = b * l
    x2 = x.reshape(n, d)

    seg = np.kron(np.eye(GROUP_W // ATTN_QKDIM, dtype=np.float32),
                  np.full((ATTN_QKDIM, ATTN_QKDIM), 1.0 / ATTN_QKDIM, np.float32))
    reps = GROUP_W // ATTN_QKDIM
    lbf = jnp.cumsum(jax.nn.softmax(lb_fwd.astype(F32), axis=0), axis=0)[0][None]
    lbb = jnp.cumsum(jax.nn.softmax(lb_bwd.astype(F32), axis=0), axis=0)[0][None]
    cs, kaug, aug = _alibi_operands()
    qt, kp, vt, rq, kf, kb, lf, lb, ri, sg = _inproj(
        x2, norm_mix_w[None], w_in.astype(BF16), jnp.asarray(seg, BF16),
        jnp.tile(q_norm_w, reps)[None], jnp.tile(k_norm_w, reps)[None], lbf, lbb, kaug, b)

    wcol = jnp.broadcast_to(attn_out_norm_w.astype(F32)[:, None], (ATTN_VDIM, ATTN_BQ))
    score_bound = (1.02 * ATTN_QKDIM ** 0.5 * LOG2E) * (jnp.max(jnp.abs(q_norm_w))
                                                         * jnp.max(jnp.abs(k_norm_w)))
    attn_args = (cs, lam.reshape(1), qt, kp.reshape(b, l, kp.shape[-1]), vt, aug, wcol)
    ao = lax.cond(2.0 * score_bound <= ATTN_FIXED_SHIFT_RANGE,
                  functools.partial(_attn, online=False),
                  functools.partial(_attn, online=True), *attn_args)

    shape3 = (b, l, GROUP_W)
    o_f, o_b = _gla(rq.reshape(shape3), kf.reshape(shape3), kb.reshape(shape3),
                    ri.reshape(shape3), lf.reshape(shape3), lb.reshape(shape3),
                    _gla_constants(GLA_CHUNK))

    x1, h2 = _outproj(x2, ao.reshape(n, GROUP_W), o_f.reshape(n, GROUP_W),
                      o_b.reshape(n, GROUP_W), sg, jnp.tile(rec_out_norm_w, REC_HEADS)[None],
                      w_out.astype(BF16), norm_ffn_w[None])
    y = _ffn(h2, x1, w_up.astype(BF16), conv_w, conv_b[None], w_down.astype(BF16), l)
    return y.reshape(b, l, d)


def kernel(x_prompt, x_sample, norm_mix_w, w_in, q_norm_w, k_norm_w, lambda_q1, lambda_k1,
           lambda_q2, lambda_k2, attn_out_norm_w, lb_fwd, lb_bwd, rec_out_norm_w, w_out,
           norm_ffn_w, w_up, conv_w, conv_b, w_down):
    assert norm_mix_w.shape[0] == 1, "single-layer trunk"
    lam = (jnp.exp(jnp.sum(lambda_q1[0].astype(F32) * lambda_k1[0].astype(F32)))
           - jnp.exp(jnp.sum(lambda_q2[0].astype(F32) * lambda_k2[0].astype(F32))) + LAM_INIT)
    params = (norm_mix_w[0], w_in[0], q_norm_w[0], k_norm_w[0], lam, attn_out_norm_w[0],
              lb_fwd, lb_bwd, rec_out_norm_w[0], w_out[0], norm_ffn_w[0], w_up[0], conv_w[0],
              conv_b[0], w_down[0])
    return (_trunk(x_prompt, *params), _trunk(x_sample, *params))
```

```python
import functools
import math

import numpy as np
import jax
import jax.numpy as jnp
from jax import lax
from jax.experimental import pallas as pl
from jax.experimental.pallas import tpu as pltpu

F32 = jnp.float32
BF16 = jnp.bfloat16

NORM_EPS = 1e-6
ATTN_HEADS = 4
ATTN_QKDIM = 64
ATTN_VDIM = 128
REC_HEADS = 4
REC_DIM = 128
GROUP_W = 512
LOG2E = 1.4426950408889634
LAM_INIT = 0.8 - 0.6 * math.exp(-0.3 * 0)

V7X_VMEM_LIMIT = 56 * 1024 * 1024

TOKEN_TILE = 512
ATTN_BQ = 256
ATTN_BK = 256
ATTN_MAX_GROUP = 31
ATTN_LOOKAHEAD = 4
ATTN_FIXED_SHIFT_RANGE = 64.0
ATTN_VROWS = ATTN_VDIM + 16
GLA_CHUNK = 128
GLA_TILE = 512
FFN_FT = 256
HALO = 16


def _cparams(sem):
    return pltpu.CompilerParams(dimension_semantics=sem, vmem_limit_bytes=V7X_VMEM_LIMIT)


def _const_spec(shape):
    zeros = (0,) * len(shape)
    return pl.BlockSpec(shape, lambda *_: zeros)


def _inproj_body(x_ref, nw_ref, w_ref, seg_ref, qw_ref, kw_ref, lbf_ref, lbb_ref,
                 kaug_ref, q_ref, k_ref, v_ref, rq_ref, kf_ref, kb_ref, lf_ref, lb_ref, ri_ref,
                 sg_ref):
    x = x_ref[...]
    ms = jnp.mean(x * x, axis=-1, keepdims=True)
    h = (x * lax.rsqrt(ms + NORM_EPS) * nw_ref[...]).astype(BF16)

    def proj(g):
        return jnp.dot(h, w_ref[:, g * GROUP_W:(g + 1) * GROUP_W], preferred_element_type=F32)

    def head_norm(p, w):
        m = jnp.dot((p * p).astype(BF16), seg_ref[...], preferred_element_type=F32)
        return p * lax.rsqrt(m + NORM_EPS) * w

    def sigmoid(p):
        return 1.0 / (1.0 + jnp.exp(-p))

    def put_q(p):
        qn = head_norm(p, qw_ref[...]) * (ATTN_QKDIM ** -0.5 * LOG2E)
        q_ref[...] = qn.T.astype(BF16)

    def put_k(p):
        kn = head_norm(p, kw_ref[...])
        low = lax.broadcasted_iota(jnp.int32, (kn.shape[0], 128), 1) < ATTN_QKDIM
        for t in range(GROUP_W // 128):
            pair = kn[:, t * 128:(t + 1) * 128]
            for half, src in ((0, pair), (1, pltpu.roll(pair, ATTN_QKDIM, axis=1))):
                cols = slice((2 * t + half) * 128, (2 * t + half + 1) * 128)
                k_ref[:, cols] = jnp.where(low, src, kaug_ref[:, cols]).astype(BF16)

    def put_v(p):
        vt = p.T
        tail = (lax.broadcasted_iota(jnp.int32, (ATTN_VROWS - ATTN_VDIM, vt.shape[-1]), 0) == 0)
        for hd in range(ATTN_HEADS):
            v_ref[hd, :ATTN_VDIM, :] = vt[hd * ATTN_VDIM:(hd + 1) * ATTN_VDIM].astype(BF16)
            v_ref[hd, ATTN_VDIM:, :] = tail.astype(BF16)

    def put_silu(ref):
        def put(p):
            ref[...] = (p * sigmoid(p)).astype(BF16)
        return put

    def put_gate(lb_in, k_out, l_out):
        def put(p):
            lbv = lb_in[...]
            f = lbv + (1.0 - lbv) * sigmoid(p)
            k_out[...] = (1.0 - f).astype(BF16)
            l_out[...] = jnp.log2(f)
        return put

    def put_ri(p):
        ri_ref[...] = p.astype(BF16)

    epilogues = (put_q, put_k, put_v, put_silu(rq_ref), put_gate(lbf_ref, kf_ref, lf_ref),
                 put_gate(lbb_ref, kb_ref, lb_ref), put_ri, put_silu(sg_ref))
    ahead = proj(0)
    for g, epilogue in enumerate(epilogues):
        p = ahead
        if g + 1 < len(epilogues):
            ahead = proj(g + 1)
        epilogue(p)


def _inproj(x2, nw, w_in, seg, qw, kw, lbf, lbb, kaug, batch):
    n, d = x2.shape
    tm = TOKEN_TILE
    l = n // batch
    tps = l // tm
    aug_tiles = kaug.shape[0] // tm
    row = lambda i: (i, 0)
    out_bf = jax.ShapeDtypeStruct((n, GROUP_W), BF16)
    out_f = jax.ShapeDtypeStruct((n, GROUP_W), F32)
    ospec = pl.BlockSpec((tm, GROUP_W), row)
    kw_cols = kaug.shape[1]
    return pl.pallas_call(
        _inproj_body,
        grid=(n // tm,),
        in_specs=[pl.BlockSpec((tm, d), row), _const_spec((1, d)), _const_spec(w_in.shape),
                  _const_spec(seg.shape)] + [_const_spec((1, GROUP_W))] * 4
                 + [pl.BlockSpec((tm, kw_cols), lambda i: (i % aug_tiles, 0))],
        out_specs=[pl.BlockSpec((None, GROUP_W, tm), lambda i: (i // tps, 0, i % tps)),
                   pl.BlockSpec((tm, kw_cols), row),
                   pl.BlockSpec((None, ATTN_HEADS, ATTN_VROWS, tm),
                                lambda i: (i // tps, 0, 0, i % tps))] + [ospec] * 7,
        out_shape=[jax.ShapeDtypeStruct((batch, GROUP_W, l), BF16),
                   jax.ShapeDtypeStruct((n, kw_cols), BF16),
                   jax.ShapeDtypeStruct((batch, ATTN_HEADS, ATTN_VROWS, l), BF16),
                   out_bf, out_bf, out_bf, out_f, out_f, out_bf, out_bf],
        compiler_params=_cparams(("parallel",)),
    )(x2, nw, w_in, seg, qw, kw, lbf, lbb, kaug)


def _attn_body(c_ref, lam_ref, qt_ref, k_ref, vt_ref, aug_ref, wcol_ref, o_ref, m_ref, acc_ref,
               *, bq, bk, nkv, group_size, online):
    hd = pl.program_id(1)
    t0 = pl.program_id(2) * bq
    c = c_ref[hd]
    lam = lam_ref[0]
    jd = lax.div(t0, bk)
    tg = t0 + lax.broadcasted_iota(jnp.int32, (1, bq), 1)

    if online:
        m_ref[...] = jnp.full(m_ref.shape, -jnp.inf, F32)
        acc_ref[...] = jnp.zeros(acc_ref.shape, F32)

    def scores(blk, mp):
        j, aug, _, bias = blk
        s0 = pl.multiple_of(j * bk, bk)
        qt = jnp.concatenate([qt_ref[mp * ATTN_QKDIM:(mp + 1) * ATTN_QKDIM, :], aug], axis=0)
        st = jnp.dot(k_ref[pl.ds(s0, bk), mp * 128:(mp + 1) * 128], qt,
                     preferred_element_type=F32)
        return st if bias is None else st + bias

    def fold(blk, mp, st):
        j, _, wcoef, bias = blk
        s0 = pl.multiple_of(j * bk, bk)
        w = wcoef * (tg - s0).astype(F32)
        vt = vt_ref[:, pl.ds(s0, bk)]
        if online:
            m_old = m_ref[mp]
            m_new = jnp.maximum(m_old, jnp.max(st, axis=0, keepdims=True) + w)
            alpha = jnp.exp2(m_old - m_new)
            p = jnp.exp2((st - (m_new - w)).astype(BF16))
            acc_ref[mp] = acc_ref[mp] * alpha + jnp.dot(vt, p, preferred_element_type=F32)
            m_ref[mp] = m_new
        elif bias is not None:
            shift = jnp.max(st, axis=0, keepdims=True)
            m_ref[mp] = shift
            p = jnp.exp2((st - shift).astype(BF16))
            acc_ref[mp] = jnp.dot(vt, p, preferred_element_type=F32)
        else:
            p = jnp.exp2((st - (m_ref[mp] - w)).astype(BF16))
            acc_ref[mp] += jnp.dot(vt, p, preferred_element_type=F32)

    def run(blocks):
        units = [(blk, mp) for blk in blocks for mp in range(2)]
        pending = [scores(*u) for u in units[:ATTN_LOOKAHEAD]]
        for n, unit in enumerate(units):
            if n + ATTN_LOOKAHEAD < len(units):
                pending.append(scores(*units[n + ATTN_LOOKAHEAD]))
            fold(*unit, pending.pop(0))

    rows = lax.broadcasted_iota(jnp.int32, (bk, bq), 0)
    dist = jnp.abs((tg - jd * bk) - rows).astype(F32)
    diag = (jd, jnp.zeros((ATTN_QKDIM, bq), BF16), 0.0, dist * (-c))

    def off_blocks(first, count):
        blocks = []
        for u in range(count):
            idx = first + u
            j = idx + (idx >= jd).astype(jnp.int32)
            sgn = jnp.where(j < jd, 1.0, -1.0)
            blocks.append((j, (aug_ref[...] * sgn).astype(BF16), -sgn * c, None))
        return blocks

    n_off = nkv - 1
    head = group_size + n_off % group_size

    def group(gi, carry):
        run(off_blocks(head + gi * group_size, group_size))
        return carry

    run([diag] + off_blocks(0, head))
    lax.fori_loop(0, n_off // group_size - 1, group, 0)

    a1 = acc_ref[0]
    a2 = acc_ref[1]
    o1 = a1[:ATTN_VDIM] * (1.0 / a1[ATTN_VDIM:ATTN_VDIM + 1])
    o2 = a2[:ATTN_VDIM] * (1.0 / a2[ATTN_VDIM:ATTN_VDIM + 1])
    o = o1 - lam * o2
    ms = jnp.mean(o * o, axis=0, keepdims=True)
    y = o * lax.rsqrt(ms + NORM_EPS) * wcol_ref[...] * (1.0 - LAM_INIT)
    o_ref[...] = y.T.astype(BF16)


def _attn(cs, lam, qt, kp, vt, aug, wcol, online):
    b, h, _, l = vt.shape
    bq, bk = ATTN_BQ, ATTN_BK
    nkv = l // bk
    assert nkv >= 2
    group_size = min(ATTN_MAX_GROUP, nkv - 1)
    body = functools.partial(_attn_body, bq=bq, bk=bk, nkv=nkv, group_size=group_size,
                             online=online)
    smem = pl.BlockSpec(memory_space=pltpu.SMEM)
    return pl.pallas_call(
        body,
        grid=(b, h, l // bq),
        in_specs=[smem, smem,
                  pl.BlockSpec((None, 2 * ATTN_QKDIM, bq), lambda bi, hi, i: (bi, hi, i)),
                  pl.BlockSpec((None, l, 256), lambda bi, hi, i: (bi, 0, hi)),
                  pl.BlockSpec((None, None, ATTN_VROWS, l), lambda bi, hi, i: (bi, hi, 0, 0)),
                  pl.BlockSpec((None, ATTN_QKDIM, bq), lambda bi, hi, i: (hi, 0, 0)),
                  _const_spec((ATTN_VDIM, bq))],
        out_specs=pl.BlockSpec((None, bq, ATTN_VDIM), lambda bi, hi, i: (bi, i, hi)),
        out_shape=jax.ShapeDtypeStruct((b, l, h * ATTN_VDIM), BF16),
        scratch_shapes=[pltpu.VMEM((2, 1, bq), F32), pltpu.VMEM((2, ATTN_VROWS, bq), F32)],
        compiler_params=_cparams(("parallel", "parallel", "arbitrary")),
    )(cs, lam, qt, kp, vt, aug, wcol)


def _gla_levels(c):
    lv, half = [], c // 2
    while half >= 8:
        lv.append(half)
        half //= 2
    return lv


def _gla_constants(c):
    r = np.arange(c)[:, None]
    u = np.arange(c)[None, :]
    mats = [(u <= r), (u > r)]
    bmasks = []
    for half in _gla_levels(c):
        blk = r // (2 * half)
        upper = (r % (2 * half)) >= half
        mid = blk * 2 * half + half - 1
        mats.append(np.where(upper, (u > mid) & (u <= r), (u > r) & (u <= mid)))
        t, s = r, u
        bmasks.append(((t // (2 * half)) == (s // (2 * half)))
                      & ((t % (2 * half)) >= half) & ((s % (2 * half)) < half))
    dmask = ((r // 8) == (u // 8)) & (u <= r)
    fwd_m = np.concatenate([m.astype(np.float32) for m in mats], axis=0)
    fwd_b = np.stack([m.astype(np.float32) for m in bmasks])
    fwd_d = dmask.astype(np.float32)
    flip = lambda m: m[::-1, ::-1]
    bwd_m = np.concatenate([flip(m.astype(np.float32)) for m in mats], axis=0)
    bwd_b = np.stack([flip(m.astype(np.float32)) for m in bmasks])
    bwd_d = flip(fwd_d)
    return (jnp.asarray(np.stack([fwd_m, bwd_m]), BF16),
            jnp.asarray(np.stack([fwd_b, bwd_b]), F32),
            jnp.asarray(np.stack([fwd_d, bwd_d]), F32))


def _gla_body(qf_ref, kf_ref, vf_ref, lf_ref, qb_ref, kb_ref, vb_ref, lb_ref,
              mst_ref, bm_ref, dm_ref, of_ref, ob_ref, s_ref, *, c, nch):
    nlev = len(_gla_levels(c))
    nt = (((1,), (1,)), ((), ()))

    @pl.when(pl.program_id(1) == 0)
    def _():
        s_ref[...] = jnp.zeros(s_ref.shape, F32)

    lane = lax.broadcasted_iota(jnp.int32, (8, c), 1)
    dirs = ((qf_ref, kf_ref, vf_ref, lf_ref, of_ref), (qb_ref, kb_ref, vb_ref, lb_ref, ob_ref))

    def chunk(ci, carry):
        exps, rows = [], []
        for d, (_, _, _, l_ref, _) in enumerate(dirs):
            cc = ci if d == 0 else nch - 1 - ci
            r0 = pl.multiple_of(cc * c, c)
            lg = l_ref[pl.ds(r0, c), :]
            h1 = lg.astype(BF16)
            h2 = (lg - h1.astype(F32)).astype(BF16)
            mst = mst_ref[d]
            exps.append(jnp.dot(mst, h1, preferred_element_type=F32)
                        + jnp.dot(mst, h2, preferred_element_type=F32))
            rows.append(r0)

        units = []
        for hd in range(REC_HEADS):
            cols = slice(hd * REC_DIM, (hd + 1) * REC_DIM)
            for d, (q_ref, k_ref, v_ref, _, _) in enumerate(dirs):
                ex, r0 = exps[d], rows[d]
                g = ex[0:c, cols]
                q = q_ref[pl.ds(r0, c), cols].astype(F32)
                k = k_ref[pl.ds(r0, c), cols].astype(F32)
                st = s_ref[d, hd]
                o = lax.dot_general((q * jnp.exp2(g)).astype(BF16), st.astype(BF16), nt,
                                    preferred_element_type=F32)
                a = jnp.zeros((c, c), F32)
                for li in range(nlev):
                    eh = jnp.exp2(ex[(2 + li) * c:(3 + li) * c, cols])
                    pr = lax.dot_general((q * eh).astype(BF16), (k * eh).astype(BF16), nt,
                                         preferred_element_type=F32)
                    a = a + bm_ref[d, li] * pr
                units.append((hd, d, cols, g, q, k, st, o, a))

        diags = []
        for hd, d, cols, g, q, k, st, o, a in units:
            strips = []
            for blk in range(c // 8):
                rs = slice(8 * blk, 8 * blk + 8)
                gb, qb, kb = g[rs], q[rs], k[rs]
                strip = jnp.zeros((8, c), F32)
                for s in range(8):
                    e = jnp.exp2(gb - gb[s:s + 1])
                    col = jnp.sum(qb * (kb[s:s + 1] * e), axis=-1, keepdims=True)
                    strip = jnp.where(lane == 8 * blk + s, col, strip)
                strips.append(strip)
            diags.append(jnp.where(dm_ref[d] > 0.0, jnp.concatenate(strips, axis=0), 0.0))

        for (hd, d, cols, g, q, k, st, o, a), dg in zip(units, diags):
            v_ref, o_ref = dirs[d][2], dirs[d][4]
            ex, r0 = exps[d], rows[d]
            v = v_ref[pl.ds(r0, c), cols]
            o = o + jnp.dot((a + dg).astype(BF16), v, preferred_element_type=F32)
            o_ref[pl.ds(r0, c), cols] = o.astype(BF16)
            kt = (k * jnp.exp2(ex[c:2 * c, cols])).astype(BF16)
            g_all = g[c - 1:c] if d == 0 else g[0:1]
            s_ref[d, hd] = st * jnp.exp2(g_all) + jnp.dot(
                v.astype(F32).T.astype(BF16), kt, preferred_element_type=F32)
        return carry

    lax.fori_loop(0, nch, chunk, 0)


def _gla(rq, kf, kb, ri, lf, lb, consts):
    b, l, w = rq.shape
    c, tc = GLA_CHUNK, GLA_TILE
    n = l // tc
    mst, bm, dm = consts
    fwd = pl.BlockSpec((None, tc, w), lambda bi, i: (bi, i, 0))
    bwd = pl.BlockSpec((None, tc, w), lambda bi, i: (bi, n - 1 - i, 0))
    out = jax.ShapeDtypeStruct((b, l, w), BF16)
    return pl.pallas_call(
        functools.partial(_gla_body, c=c, nch=tc // c),
        grid=(b, n),
        in_specs=[fwd, fwd, fwd, fwd, bwd, bwd, bwd, bwd,
                  _const_spec(mst.shape), _const_spec(bm.shape), _const_spec(dm.shape)],
        out_specs=[fwd, bwd],
        out_shape=[out, out],
        scratch_shapes=[pltpu.VMEM((2, REC_HEADS, REC_DIM, REC_DIM), F32)],
        compiler_params=_cparams(("parallel", "arbitrary")),
    )(rq, kf, ri, lf, rq, kb, ri, lb, mst, bm, dm)


def _mix_residual(x, ao, o_f, o_b, sg, rw_ref, wo_ref):
    ro = o_f.astype(F32) + o_b.astype(F32)
    parts = []
    for hd in range(REC_HEADS):
        seg = ro[:, hd * REC_DIM:(hd + 1) * REC_DIM]
        ms = jnp.mean(seg * seg, axis=-1, keepdims=True)
        parts.append(seg * lax.rsqrt(ms + NORM_EPS))
    ron = jnp.concatenate(parts, axis=-1) * rw_ref[...] * sg.astype(F32)
    aw = ao.shape[-1]
    mix = (jnp.dot(ao, wo_ref[:aw, :], preferred_element_type=F32)
           + jnp.dot(ron.astype(BF16), wo_ref[aw:, :], preferred_element_type=F32))
    return x + mix


def _mix_ffn_body(*refs, tm, ft, dff, tiles_per_seq):
    rows = [refs[3 * a:3 * a + 3] for a in range(5)]
    (rw_ref, wo_ref, fw_ref, wu_ref, cw_ref, cb_ref, wd_ref, y_ref, acc_ref, u_ref) = refs[15:]
    x, ao, o_f, o_b, sg = [jnp.concatenate([r[...] for r in trio], axis=0) for trio in rows]
    ext = tm + 2 * HALO
    x1 = _mix_residual(x, ao, o_f, o_b, sg, rw_ref, wo_ref)
    ms = jnp.mean(x1 * x1, axis=-1, keepdims=True)
    h2 = x1 * lax.rsqrt(ms + NORM_EPS) * fw_ref[...]
    pos = lax.rem(pl.program_id(0), tiles_per_seq)
    r = lax.broadcasted_iota(jnp.int32, (ext, 1), 0)
    keep = (((r >= HALO) | (pos != 0)) & ((r < HALO + tm) | (pos != tiles_per_seq - 1)))
    hx = jnp.where(keep, h2, 0.0).astype(BF16)
    acc_ref[...] = x1[HALO:HALO + tm]

    def conv(u, col):
        prev = pltpu.roll(u, 1, axis=0)[HALO:HALO + tm]
        nxt = pltpu.roll(u, ext - 1, axis=0)[HALO:HALO + tm]
        cw = cw_ref[:, pl.ds(col, ft)]
        return (cb_ref[:, pl.ds(col, ft)] + prev * cw[0:1] + u[HALO:HALO + tm] * cw[1:2]
                + nxt * cw[2:3])

    def up_dots(f, slot):
        cg = pl.multiple_of(f * ft, ft)
        cu = pl.multiple_of(dff + f * ft, ft)
        u_ref[slot, 0] = jnp.dot(hx, wu_ref[:, pl.ds(cg, ft)], preferred_element_type=F32)
        u_ref[slot, 1] = jnp.dot(hx, wu_ref[:, pl.ds(cu, ft)], preferred_element_type=F32)

    def finish(f, slot):
        cg = pl.multiple_of(f * ft, ft)
        cu = pl.multiple_of(dff + f * ft, ft)
        gate = conv(u_ref[slot, 0], cg)
        up = conv(u_ref[slot, 1], cu)
        act = (gate * (1.0 / (1.0 + jnp.exp(-gate))) * up).astype(BF16)
        acc_ref[...] += jnp.dot(act, wd_ref[pl.ds(cg, ft), :], preferred_element_type=F32)

    def pair(i, carry):
        f = 2 * i
        up_dots(f + 1, 1)
        finish(f, 0)
        up_dots(f + 2, 0)
        finish(f + 1, 1)
        return carry

    nf = dff // ft
    up_dots(0, 0)
    lax.fori_loop(0, (nf - 1) // 2, pair, 0)
    for f in range(2 * ((nf - 1) // 2), nf):
        if f + 1 < nf:
            up_dots(f + 1, (f + 1) % 2)
        finish(f, f % 2)
    y_ref[...] = acc_ref[...]


def _mix_ffn(x2, ao, o_f, o_b, sg, rw, w_out, fw, w_up, conv_w, conv_b, w_down, seq_len):
    n, d = x2.shape
    tm, ft = TOKEN_TILE, FFN_FT
    dff = w_down.shape[0]
    hb = tm // HALO
    nhb = n // HALO

    def trio(width):
        return [pl.BlockSpec((HALO, width), lambda i: (jnp.maximum(i * hb - 1, 0), 0)),
                pl.BlockSpec((tm, width), lambda i: (i, 0)),
                pl.BlockSpec((HALO, width), lambda i: (jnp.minimum((i + 1) * hb, nhb - 1), 0))]

    body = functools.partial(_mix_ffn_body, tm=tm, ft=ft, dff=dff, tiles_per_seq=seq_len // tm)
    row_inputs = (x2, ao, o_f, o_b, sg)
    consts = (rw, w_out, fw, w_up, conv_w, conv_b, w_down)
    return pl.pallas_call(
        body,
        grid=(n // tm,),
        in_specs=[spec for a in row_inputs for spec in trio(a.shape[1])]
                 + [_const_spec(a.shape) for a in consts],
        out_specs=pl.BlockSpec((tm, d), lambda i: (i, 0)),
        out_shape=jax.ShapeDtypeStruct((n, d), F32),
        scratch_shapes=[pltpu.VMEM((tm, d), F32), pltpu.VMEM((2, 2, tm + 2 * HALO, ft), F32)],
        compiler_params=_cparams(("parallel",)),
    )(*[a for a in row_inputs for _ in range(3)], *consts)


def _bf16_split3(x):
    x = x.astype(F32)
    a = x.astype(BF16)
    r = x - a.astype(F32)
    b = r.astype(BF16)
    c = (r - b.astype(F32)).astype(BF16)
    return a, b, c


def _alibi_operands():
    start = 2.0 ** (-8.0 / ATTN_HEADS)
    slopes = np.array([start ** (i + 1) for i in range(ATTN_HEADS)], np.float32)
    cs = jnp.asarray(slopes * np.float32(LOG2E), F32)
    groups = GROUP_W // ATTN_QKDIM
    nrows = max(ATTN_BK, TOKEN_TILE)
    pos = np.arange(nrows) % ATTN_BK
    kaug = np.zeros((nrows, groups, 128), np.float32)
    kaug[:, :, ATTN_QKDIM:ATTN_QKDIM + 3] = (pos - pos % 16)[:, None, None]
    kaug[:, :, ATTN_QKDIM + 3:ATTN_QKDIM + 6] = (pos % 16)[:, None, None]
    c1, c2, c3 = _bf16_split3(cs)
    six = jnp.stack([c1, c2, c3, c1, c2, c3], axis=1).astype(F32)
    rows = jnp.zeros((ATTN_HEADS, ATTN_QKDIM), F32).at[:, :6].set(six)
    aug = jnp.broadcast_to(rows[..., None], rows.shape + (ATTN_BQ,))
    return cs, jnp.asarray(kaug.reshape(nrows, groups * 128), F32), aug


def _trunk(x, norm_mix_w, w_in, q_norm_w, k_norm_w, lam, attn_out_norm_w, lb_fwd, lb_bwd,
           rec_out_norm_w, w_out, norm_ffn_w, w_up, conv_w, conv_b, w_down):
    b, l, d = x.shape
    n = b * l
    x2 = x.reshape(n, d)

    seg = np.kron(np.eye(GROUP_W // ATTN_QKDIM, dtype=np.float32),
                  np.full((ATTN_QKDIM, ATTN_QKDIM), 1.0 / ATTN_QKDIM, np.float32))
    reps = GROUP_W // ATTN_QKDIM
    lbf = jnp.cumsum(jax.nn.softmax(lb_fwd.astype(F32), axis=0), axis=0)[0][None]
    lbb = jnp.cumsum(jax.nn.softmax(lb_bwd.astype(F32), axis=0), axis=0)[0][None]
    cs, kaug, aug = _alibi_operands()
    qt, kp, vt, rq, kf, kb, lf, lb, ri, sg = _inproj(
        x2, norm_mix_w[None], w_in.astype(BF16), jnp.asarray(seg, BF16),
        jnp.tile(q_norm_w, reps)[None], jnp.tile(k_norm_w, reps)[None], lbf, lbb, kaug, b)

    wcol = jnp.broadcast_to(attn_out_norm_w.astype(F32)[:, None], (ATTN_VDIM, ATTN_BQ))
    score_bound = (1.02 * ATTN_QKDIM ** 0.5 * LOG2E) * (jnp.max(jnp.abs(q_norm_w))
                                                         * jnp.max(jnp.abs(k_norm_w)))
    attn_args = (cs, lam.reshape(1), qt, kp.reshape(b, l, kp.shape[-1]), vt, aug, wcol)
    ao = lax.cond(2.0 * score_bound <= ATTN_FIXED_SHIFT_RANGE,
                  functools.partial(_attn, online=False),
                  functools.partial(_attn, online=True), *attn_args)

    shape3 = (b, l, GROUP_W)
    o_f, o_b = _gla(rq.reshape(shape3), kf.reshape(shape3), kb.reshape(shape3),
                    ri.reshape(shape3), lf.reshape(shape3), lb.reshape(shape3),
                    _gla_constants(GLA_CHUNK))

    y = _mix_ffn(x2, ao.reshape(n, GROUP_W), o_f.reshape(n, GROUP_W), o_b.reshape(n, GROUP_W),
                 sg, jnp.tile(rec_out_norm_w, REC_HEADS)[None], w_out.astype(BF16),
                 norm_ffn_w[None], w_up.astype(BF16), conv_w, conv_b[None], w_down.astype(BF16), l)
    return y.reshape(b, l, d)


def kernel(x_prompt, x_sample, norm_mix_w, w_in, q_norm_w, k_norm_w, lambda_q1, lambda_k1,
           lambda_q2, lambda_k2, attn_out_norm_w, lb_fwd, lb_bwd, rec_out_norm_w, w_out,
           norm_ffn_w, w_up, conv_w, conv_b, w_down):
    assert norm_mix_w.shape[0] == 1, "single-layer trunk"
    lam = (jnp.exp(jnp.sum(lambda_q1[0].astype(F32) * lambda_k1[0].astype(F32)))
           - jnp.exp(jnp.sum(lambda_q2[0].astype(F32) * lambda_k2[0].astype(F32))) + LAM_INIT)
    params = (norm_mix_w[0], w_in[0], q_norm_w[0], k_norm_w[0], lam, attn_out_norm_w[0],
              lb_fwd, lb_bwd, rec_out_norm_w[0], w_out[0], norm_ffn_w[0], w_up[0], conv_w[0],
              conv_b[0], w_down[0])
    return (_trunk(x_prompt, *params), _trunk(x_sample, *params))
```

```python
import functools
import math

import numpy as np
import jax
import jax.numpy as jnp
from jax import lax
from jax.experimental import pallas as pl
from jax.experimental.pallas import tpu as pltpu

F32 = jnp.float32
BF16 = jnp.bfloat16

NORM_EPS = 1e-6
ATTN_HEADS = 4
ATTN_QKDIM = 64
ATTN_VDIM = 128
REC_HEADS = 4
REC_DIM = 128
GROUP_W = 512
LOG2E = 1.4426950408889634
LAM_INIT = 0.8 - 0.6 * math.exp(-0.3 * 0)

V7X_VMEM_LIMIT = 56 * 1024 * 1024

TOKEN_TILE = 512
ATTN_BQ = 256
ATTN_BK = 256
ATTN_MAX_GROUP = 31
ATTN_LOOKAHEAD = 4
ATTN_FIXED_SHIFT_RANGE = 64.0
ATTN_VROWS = ATTN_VDIM + 16
GLA_CHUNK = 128
GLA_FACTORED_DIAG = 16
GLA_FACTORED_RANGE = 96.0
GLA_TILE = 512
FFN_FT = 256
HALO = 16


def _cparams(sem):
    return pltpu.CompilerParams(dimension_semantics=sem, vmem_limit_bytes=V7X_VMEM_LIMIT)


def _const_spec(shape):
    zeros = (0,) * len(shape)
    return pl.BlockSpec(shape, lambda *_: zeros)


def _inproj_body(x_ref, nw_ref, w_ref, seg_ref, qw_ref, kw_ref, lbf_ref, lbb_ref,
                 kaug_ref, q_ref, k_ref, v_ref, rq_ref, kf_ref, kb_ref, lf_ref, lb_ref, ri_ref,
                 sg_ref):
    x = x_ref[...]
    ms = jnp.mean(x * x, axis=-1, keepdims=True)
    h = (x * lax.rsqrt(ms + NORM_EPS) * nw_ref[...]).astype(BF16)

    def proj(g):
        return jnp.dot(h, w_ref[:, g * GROUP_W:(g + 1) * GROUP_W], preferred_element_type=F32)

    def head_norm(p, w):
        m = jnp.dot((p * p).astype(BF16), seg_ref[...], preferred_element_type=F32)
        return p * lax.rsqrt(m + NORM_EPS) * w

    def sigmoid(p):
        return 1.0 / (1.0 + jnp.exp(-p))

    def put_q(p):
        qn = head_norm(p, qw_ref[...]) * (ATTN_QKDIM ** -0.5 * LOG2E)
        q_ref[...] = qn.T.astype(BF16)

    def put_k(p):
        kn = head_norm(p, kw_ref[...])
        low = lax.broadcasted_iota(jnp.int32, (kn.shape[0], 128), 1) < ATTN_QKDIM
        for t in range(GROUP_W // 128):
            pair = kn[:, t * 128:(t + 1) * 128]
            for half, src in ((0, pair), (1, pltpu.roll(pair, ATTN_QKDIM, axis=1))):
                cols = slice((2 * t + half) * 128, (2 * t + half + 1) * 128)
                k_ref[:, cols] = jnp.where(low, src, kaug_ref[:, cols]).astype(BF16)

    def put_v(p):
        vt = p.T
        tail = (lax.broadcasted_iota(jnp.int32, (ATTN_VROWS - ATTN_VDIM, vt.shape[-1]), 0) == 0)
        for hd in range(ATTN_HEADS):
            v_ref[hd, :ATTN_VDIM, :] = vt[hd * ATTN_VDIM:(hd + 1) * ATTN_VDIM].astype(BF16)
            v_ref[hd, ATTN_VDIM:, :] = tail.astype(BF16)

    def put_silu(ref):
        def put(p):
            ref[...] = (p * sigmoid(p)).astype(BF16)
        return put

    def put_gate(lb_in, k_out, l_out):
        def put(p):
            lbv = lb_in[...]
            f = lbv + (1.0 - lbv) * sigmoid(p)
            k_out[...] = (1.0 - f).astype(BF16)
            l_out[...] = jnp.log2(f)
        return put

    def put_ri(p):
        ri_ref[...] = p.astype(BF16)

    epilogues = (put_q, put_k, put_v, put_silu(rq_ref), put_gate(lbf_ref, kf_ref, lf_ref),
                 put_gate(lbb_ref, kb_ref, lb_ref), put_ri, put_silu(sg_ref))
    ahead = proj(0)
    for g, epilogue in enumerate(epilogues):
        p = ahead
        if g + 1 < len(epilogues):
            ahead = proj(g + 1)
        epilogue(p)


def _inproj(x2, nw, w_in, seg, qw, kw, lbf, lbb, kaug, batch):
    n, d = x2.shape
    tm = TOKEN_TILE
    l = n // batch
    tps = l // tm
    aug_tiles = kaug.shape[0] // tm
    row = lambda i: (i, 0)
    out_bf = jax.ShapeDtypeStruct((n, GROUP_W), BF16)
    out_f = jax.ShapeDtypeStruct((n, GROUP_W), F32)
    ospec = pl.BlockSpec((tm, GROUP_W), row)
    kw_cols = kaug.shape[1]
    return pl.pallas_call(
        _inproj_body,
        grid=(n // tm,),
        in_specs=[pl.BlockSpec((tm, d), row), _const_spec((1, d)), _const_spec(w_in.shape),
                  _const_spec(seg.shape)] + [_const_spec((1, GROUP_W))] * 4
                 + [pl.BlockSpec((tm, kw_cols), lambda i: (i % aug_tiles, 0))],
        out_specs=[pl.BlockSpec((None, GROUP_W, tm), lambda i: (i // tps, 0, i % tps)),
                   pl.BlockSpec((tm, kw_cols), row),
                   pl.BlockSpec((None, ATTN_HEADS, ATTN_VROWS, tm),
                                lambda i: (i // tps, 0, 0, i % tps))] + [ospec] * 7,
        out_shape=[jax.ShapeDtypeStruct((batch, GROUP_W, l), BF16),
                   jax.ShapeDtypeStruct((n, kw_cols), BF16),
                   jax.ShapeDtypeStruct((batch, ATTN_HEADS, ATTN_VROWS, l), BF16),
                   out_bf, out_bf, out_bf, out_f, out_f, out_bf, out_bf],
        compiler_params=_cparams(("parallel",)),
    )(x2, nw, w_in, seg, qw, kw, lbf, lbb, kaug)


def _attn_body(c_ref, lam_ref, qt_ref, k_ref, vt_ref, aug_ref, wcol_ref, o_ref, m_ref, acc_ref,
               *, bq, bk, nkv, group_size, online):
    hd = pl.program_id(1)
    t0 = pl.program_id(2) * bq
    c = c_ref[hd]
    lam = lam_ref[0]
    jd = lax.div(t0, bk)
    tg = t0 + lax.broadcasted_iota(jnp.int32, (1, bq), 1)

    if online:
        m_ref[...] = jnp.full(m_ref.shape, -jnp.inf, F32)
        acc_ref[...] = jnp.zeros(acc_ref.shape, F32)

    def scores(blk, mp):
        j, aug, _, bias = blk
        s0 = pl.multiple_of(j * bk, bk)
        qt = jnp.concatenate([qt_ref[mp * ATTN_QKDIM:(mp + 1) * ATTN_QKDIM, :], aug], axis=0)
        st = jnp.dot(k_ref[pl.ds(s0, bk), mp * 128:(mp + 1) * 128], qt,
                     preferred_element_type=F32)
        return st if bias is None else st + bias

    def fold(blk, mp, st):
        j, _, wcoef, bias = blk
        s0 = pl.multiple_of(j * bk, bk)
        w = wcoef * (tg - s0).astype(F32)
        vt = vt_ref[:, pl.ds(s0, bk)]
        if online:
            m_old = m_ref[mp]
            m_new = jnp.maximum(m_old, jnp.max(st, axis=0, keepdims=True) + w)
            alpha = jnp.exp2(m_old - m_new)
            p = jnp.exp2((st - (m_new - w)).astype(BF16))
            acc_ref[mp] = acc_ref[mp] * alpha + jnp.dot(vt, p, preferred_element_type=F32)
            m_ref[mp] = m_new
        elif bias is not None:
            shift = jnp.max(st, axis=0, keepdims=True)
            m_ref[mp] = shift
            p = jnp.exp2((st - shift).astype(BF16))
            acc_ref[mp] = jnp.dot(vt, p, preferred_element_type=F32)
        else:
            p = jnp.exp2((st - (m_ref[mp] - w)).astype(BF16))
            acc_ref[mp] += jnp.dot(vt, p, preferred_element_type=F32)

    def run(blocks):
        units = [(blk, mp) for blk in blocks for mp in range(2)]
        pending = [scores(*u) for u in units[:ATTN_LOOKAHEAD]]
        for n, unit in enumerate(units):
            if n + ATTN_LOOKAHEAD < len(units):
                pending.append(scores(*units[n + ATTN_LOOKAHEAD]))
            fold(*unit, pending.pop(0))

    rows = lax.broadcasted_iota(jnp.int32, (bk, bq), 0)
    dist = jnp.abs((tg - jd * bk) - rows).astype(F32)
    diag = (jd, jnp.zeros((ATTN_QKDIM, bq), BF16), 0.0, dist * (-c))

    def off_blocks(first, count):
        blocks = []
        for u in range(count):
            idx = first + u
            j = idx + (idx >= jd).astype(jnp.int32)
            sgn = jnp.where(j < jd, 1.0, -1.0)
            blocks.append((j, (aug_ref[...] * sgn).astype(BF16), -sgn * c, None))
        return blocks

    n_off = nkv - 1
    head = group_size + n_off % group_size

    def group(gi, carry):
        run(off_blocks(head + gi * group_size, group_size))
        return carry

    run([diag] + off_blocks(0, head))
    lax.fori_loop(0, n_off // group_size - 1, group, 0)

    a1 = acc_ref[0]
    a2 = acc_ref[1]
    o1 = a1[:ATTN_VDIM] * (1.0 / a1[ATTN_VDIM:ATTN_VDIM + 1])
    o2 = a2[:ATTN_VDIM] * (1.0 / a2[ATTN_VDIM:ATTN_VDIM + 1])
    o = o1 - lam * o2
    ms = jnp.mean(o * o, axis=0, keepdims=True)
    y = o * lax.rsqrt(ms + NORM_EPS) * wcol_ref[...] * (1.0 - LAM_INIT)
    o_ref[...] = y.T.astype(BF16)


def _attn(cs, lam, qt, kp, vt, aug, wcol, online):
    b, h, _, l = vt.shape
    bq, bk = ATTN_BQ, ATTN_BK
    nkv = l // bk
    assert nkv >= 2
    group_size = min(ATTN_MAX_GROUP, nkv - 1)
    body = functools.partial(_attn_body, bq=bq, bk=bk, nkv=nkv, group_size=group_size,
                             online=online)
    smem = pl.BlockSpec(memory_space=pltpu.SMEM)
    return pl.pallas_call(
        body,
        grid=(b, h, l // bq),
        in_specs=[smem, smem,
                  pl.BlockSpec((None, 2 * ATTN_QKDIM, bq), lambda bi, hi, i: (bi, hi, i)),
                  pl.BlockSpec((None, l, 256), lambda bi, hi, i: (bi, 0, hi)),
                  pl.BlockSpec((None, None, ATTN_VROWS, l), lambda bi, hi, i: (bi, hi, 0, 0)),
                  pl.BlockSpec((None, ATTN_QKDIM, bq), lambda bi, hi, i: (hi, 0, 0)),
                  _const_spec((ATTN_VDIM, bq))],
        out_specs=pl.BlockSpec((None, bq, ATTN_VDIM), lambda bi, hi, i: (bi, i, hi)),
        out_shape=jax.ShapeDtypeStruct((b, l, h * ATTN_VDIM), BF16),
        scratch_shapes=[pltpu.VMEM((2, 1, bq), F32), pltpu.VMEM((2, ATTN_VROWS, bq), F32)],
        compiler_params=_cparams(("parallel", "parallel", "arbitrary")),
    )(cs, lam, qt, kp, vt, aug, wcol)


def _gla_levels(c, diag):
    lv, half = [], c // 2
    while half >= diag:
        lv.append(half)
        half //= 2
    return lv


def _gla_constants(c, diag):
    r = np.arange(c)[:, None]
    u = np.arange(c)[None, :]
    mats = [(u <= r), (u > r)]
    bmasks = []
    for half in _gla_levels(c, diag):
        blk = r // (2 * half)
        upper = (r % (2 * half)) >= half
        mid = blk * 2 * half + half - 1
        mats.append(np.where(upper, (u > mid) & (u <= r), (u > r) & (u <= mid)))
        t, s = r, u
        bmasks.append(((t // (2 * half)) == (s // (2 * half)))
                      & ((t % (2 * half)) >= half) & ((s % (2 * half)) < half))
    if diag > 8:
        mats.append((u // diag == r // diag) & (u <= r))
    dmask = ((r // diag) == (u // diag)) & (u <= r)
    fwd_m = np.concatenate([m.astype(np.float32) for m in mats], axis=0)
    fwd_b = np.stack([m.astype(np.float32) for m in bmasks])
    fwd_d = dmask.astype(np.float32)
    flip = lambda m: m[::-1, ::-1]
    bwd_m = np.concatenate([flip(m.astype(np.float32)) for m in mats], axis=0)
    bwd_b = np.stack([flip(m.astype(np.float32)) for m in bmasks])
    bwd_d = flip(fwd_d)
    fwd_m, bwd_m = (np.concatenate([m, m], axis=1) for m in (fwd_m, bwd_m))
    return (jnp.asarray(np.stack([fwd_m, bwd_m]), BF16),
            jnp.asarray(np.stack([fwd_b, bwd_b]), F32),
            jnp.asarray(np.stack([fwd_d, bwd_d]), F32))


def _gla_body(qf_ref, kf_ref, vf_ref, lf_ref, qb_ref, kb_ref, vb_ref, lb_ref,
              mst_ref, bm_ref, dm_ref, of_ref, ob_ref, s_ref, *, c, nch, diag):
    nlev = len(_gla_levels(c, diag))
    nt = (((1,), (1,)), ((), ()))

    @pl.when(pl.program_id(1) == 0)
    def _():
        s_ref[...] = jnp.zeros(s_ref.shape, F32)

    lane = lax.broadcasted_iota(jnp.int32, (8, c), 1)
    dirs = ((qf_ref, kf_ref, vf_ref, lf_ref, of_ref), (qb_ref, kb_ref, vb_ref, lb_ref, ob_ref))

    def chunk(ci, carry):
        exps, rows = [], []
        for d, (_, _, _, l_ref, _) in enumerate(dirs):
            cc = ci if d == 0 else nch - 1 - ci
            r0 = pl.multiple_of(cc * c, c)
            lg = l_ref[pl.ds(r0, c), :]
            h1 = lg.astype(BF16)
            h2 = (lg - h1.astype(F32)).astype(BF16)
            exps.append(jnp.dot(mst_ref[d], jnp.concatenate([h1, h2], axis=0),
                                preferred_element_type=F32))
            rows.append(r0)

        units = []
        for hd in range(REC_HEADS):
            cols = slice(hd * REC_DIM, (hd + 1) * REC_DIM)
            for d, (q_ref, k_ref, v_ref, _, _) in enumerate(dirs):
                ex, r0 = exps[d], rows[d]
                g = ex[0:c, cols]
                q = q_ref[pl.ds(r0, c), cols].astype(F32)
                k = k_ref[pl.ds(r0, c), cols].astype(F32)
                st = s_ref[d, hd]
                o = lax.dot_general((q * jnp.exp2(g)).astype(BF16), st.astype(BF16), nt,
                                    preferred_element_type=F32)
                a = jnp.zeros((c, c), F32)
                for li in range(nlev):
                    eh = jnp.exp2(ex[(2 + li) * c:(3 + li) * c, cols])
                    pr = lax.dot_general((q * eh).astype(BF16), (k * eh).astype(BF16), nt,
                                         preferred_element_type=F32)
                    a = a + bm_ref[d, li] * pr
                if diag > 8:
                    ed = ex[(2 + nlev) * c:(3 + nlev) * c, cols]
                    pr = lax.dot_general((q * jnp.exp2(ed)).astype(BF16),
                                         (k * jnp.exp2(-ed)).astype(BF16), nt,
                                         preferred_element_type=F32)
                    a = a + jnp.where(dm_ref[d] > 0.0, pr, 0.0)
                units.append((hd, d, cols, g, q, k, st, o, a))

        diags = []
        for hd, d, cols, g, q, k, st, o, a in units:
            if diag > 8:
                diags.append(None)
                continue
            strips = []
            for blk in range(c // 8):
                rs = slice(8 * blk, 8 * blk + 8)
                gb, qb, kb = g[rs], q[rs], k[rs]
                strip = jnp.zeros((8, c), F32)
                for s in range(8):
                    e = jnp.exp2(gb - gb[s:s + 1])
                    col = jnp.sum(qb * (kb[s:s + 1] * e), axis=-1, keepdims=True)
                    strip = jnp.where(lane == 8 * blk + s, col, strip)
                strips.append(strip)
            diags.append(jnp.where(dm_ref[d] > 0.0, jnp.concatenate(strips, axis=0), 0.0))

        for (hd, d, cols, g, q, k, st, o, a), dg in zip(units, diags):
            v_ref, o_ref = dirs[d][2], dirs[d][4]
            ex, r0 = exps[d], rows[d]
            v = v_ref[pl.ds(r0, c), cols]
            if dg is not None:
                a = a + dg
            o = o + jnp.dot(a.astype(BF16), v, preferred_element_type=F32)
            o_ref[pl.ds(r0, c), cols] = o.astype(BF16)
            kt = (k * jnp.exp2(ex[c:2 * c, cols])).astype(BF16)
            g_all = g[c - 1:c] if d == 0 else g[0:1]
            s_ref[d, hd] = st * jnp.exp2(g_all) + jnp.dot(
                v.astype(F32).T.astype(BF16), kt, preferred_element_type=F32)
        return carry

    lax.fori_loop(0, nch, chunk, 0, unroll=2)


def _gla(rq, kf, kb, ri, lf, lb, diag):
    b, l, w = rq.shape
    c, tc = GLA_CHUNK, GLA_TILE
    n = l // tc
    mst, bm, dm = _gla_constants(c, diag)
    fwd = pl.BlockSpec((None, tc, w), lambda bi, i: (bi, i, 0))
    bwd = pl.BlockSpec((None, tc, w), lambda bi, i: (bi, n - 1 - i, 0))
    out = jax.ShapeDtypeStruct((b, l, w), BF16)
    return pl.pallas_call(
        functools.partial(_gla_body, c=c, nch=tc // c, diag=diag),
        grid=(b, n),
        in_specs=[fwd, fwd, fwd, fwd, bwd, bwd, bwd, bwd,
                  _const_spec(mst.shape), _const_spec(bm.shape), _const_spec(dm.shape)],
        out_specs=[fwd, bwd],
        out_shape=[out, out],
        scratch_shapes=[pltpu.VMEM((2, REC_HEADS, REC_DIM, REC_DIM), F32)],
        compiler_params=_cparams(("parallel", "arbitrary")),
    )(rq, kf, ri, lf, rq, kb, ri, lb, mst, bm, dm)


def _mix_residual(x, ao, o_f, o_b, sg, rw_ref, wo_ref):
    ro = o_f.astype(F32) + o_b.astype(F32)
    parts = []
    for hd in range(REC_HEADS):
        seg = ro[:, hd * REC_DIM:(hd + 1) * REC_DIM]
        ms = jnp.mean(seg * seg, axis=-1, keepdims=True)
        parts.append(seg * lax.rsqrt(ms + NORM_EPS))
    ron = jnp.concatenate(parts, axis=-1) * rw_ref[...] * sg.astype(F32)
    aw = ao.shape[-1]
    mix = (jnp.dot(ao, wo_ref[:aw, :], preferred_element_type=F32)
           + jnp.dot(ron.astype(BF16), wo_ref[aw:, :], preferred_element_type=F32))
    return x + mix


def _mix_ffn_body(*refs, tm, ft, dff, tiles_per_seq):
    rows = [refs[3 * a:3 * a + 3] for a in range(5)]
    (rw_ref, wo_ref, fw_ref, wu_ref, cw_ref, cb_ref, wd_ref, y_ref, acc_ref, u_ref) = refs[15:]
    x, ao, o_f, o_b, sg = [jnp.concatenate([r[...] for r in trio], axis=0) for trio in rows]
    ext = tm + 2 * HALO
    x1 = _mix_residual(x, ao, o_f, o_b, sg, rw_ref, wo_ref)
    ms = jnp.mean(x1 * x1, axis=-1, keepdims=True)
    h2 = x1 * lax.rsqrt(ms + NORM_EPS) * fw_ref[...]
    pos = lax.rem(pl.program_id(0), tiles_per_seq)
    r = lax.broadcasted_iota(jnp.int32, (ext, 1), 0)
    keep = (((r >= HALO) | (pos != 0)) & ((r < HALO + tm) | (pos != tiles_per_seq - 1)))
    hx = jnp.where(keep, h2, 0.0).astype(BF16)
    acc_ref[...] = x1[HALO:HALO + tm]

    def conv(u, col):
        prev = pltpu.roll(u, 1, axis=0)[HALO:HALO + tm]
        nxt = pltpu.roll(u, ext - 1, axis=0)[HALO:HALO + tm]
        cw = cw_ref[:, pl.ds(col, ft)]
        return (cb_ref[:, pl.ds(col, ft)] + prev * cw[0:1] + u[HALO:HALO + tm] * cw[1:2]
                + nxt * cw[2:3])

    def up_dots(f, slot):
        cg = pl.multiple_of(f * ft, ft)
        cu = pl.multiple_of(dff + f * ft, ft)
        u_ref[slot, 0] = jnp.dot(hx, wu_ref[:, pl.ds(cg, ft)], preferred_element_type=F32)
        u_ref[slot, 1] = jnp.dot(hx, wu_ref[:, pl.ds(cu, ft)], preferred_element_type=F32)

    def finish(f, slot):
        cg = pl.multiple_of(f * ft, ft)
        cu = pl.multiple_of(dff + f * ft, ft)
        gate = conv(u_ref[slot, 0], cg)
        up = conv(u_ref[slot, 1], cu)
        act = (gate * (1.0 / (1.0 + jnp.exp(-gate))) * up).astype(BF16)
        acc_ref[...] += jnp.dot(act, wd_ref[pl.ds(cg, ft), :], preferred_element_type=F32)

    def pair(i, carry):
        f = 2 * i
        up_dots(f + 1, 1)
        finish(f, 0)
        up_dots(f + 2, 0)
        finish(f + 1, 1)
        return carry

    nf = dff // ft
    up_dots(0, 0)
    lax.fori_loop(0, (nf - 1) // 2, pair, 0)
    for f in range(2 * ((nf - 1) // 2), nf):
        if f + 1 < nf:
            up_dots(f + 1, (f + 1) % 2)
        finish(f, f % 2)
    y_ref[...] = acc_ref[...]


def _mix_ffn(x2, ao, o_f, o_b, sg, rw, w_out, fw, w_up, conv_w, conv_b, w_down, seq_len):
    n, d = x2.shape
    tm, ft = TOKEN_TILE, FFN_FT
    dff = w_down.shape[0]
    hb = tm // HALO
    nhb = n // HALO

    def trio(width):
        return [pl.BlockSpec((HALO, width), lambda i: (jnp.maximum(i * hb - 1, 0), 0)),
                pl.BlockSpec((tm, width), lambda i: (i, 0)),
                pl.BlockSpec((HALO, width), lambda i: (jnp.minimum((i + 1) * hb, nhb - 1), 0))]

    body = functools.partial(_mix_ffn_body, tm=tm, ft=ft, dff=dff, tiles_per_seq=seq_len // tm)
    row_inputs = (x2, ao, o_f, o_b, sg)
    consts = (rw, w_out, fw, w_up, conv_w, conv_b, w_down)
    return pl.pallas_call(
        body,
        grid=(n // tm,),
        in_specs=[spec for a in row_inputs for spec in trio(a.shape[1])]
                 + [_const_spec(a.shape) for a in consts],
        out_specs=pl.BlockSpec((tm, d), lambda i: (i, 0)),
        out_shape=jax.ShapeDtypeStruct((n, d), F32),
        scratch_shapes=[pltpu.VMEM((tm, d), F32), pltpu.VMEM((2, 2, tm + 2 * HALO, ft), F32)],
        compiler_params=_cparams(("parallel",)),
    )(*[a for a in row_inputs for _ in range(3)], *consts)


def _bf16_split3(x):
    x = x.astype(F32)
    a = x.astype(BF16)
    r = x - a.astype(F32)
    b = r.astype(BF16)
    c = (r - b.astype(F32)).astype(BF16)
    return a, b, c


def _alibi_operands():
    start = 2.0 ** (-8.0 / ATTN_HEADS)
    slopes = np.array([start ** (i + 1) for i in range(ATTN_HEADS)], np.float32)
    cs = jnp.asarray(slopes * np.float32(LOG2E), F32)
    groups = GROUP_W // ATTN_QKDIM
    nrows = max(ATTN_BK, TOKEN_TILE)
    pos = np.arange(nrows) % ATTN_BK
    kaug = np.zeros((nrows, groups, 128), np.float32)
    kaug[:, :, ATTN_QKDIM:ATTN_QKDIM + 3] = (pos - pos % 16)[:, None, None]
    kaug[:, :, ATTN_QKDIM + 3:ATTN_QKDIM + 6] = (pos % 16)[:, None, None]
    c1, c2, c3 = _bf16_split3(cs)
    six = jnp.stack([c1, c2, c3, c1, c2, c3], axis=1).astype(F32)
    rows = jnp.zeros((ATTN_HEADS, ATTN_QKDIM), F32).at[:, :6].set(six)
    aug = jnp.broadcast_to(rows[..., None], rows.shape + (ATTN_BQ,))
    return cs, jnp.asarray(kaug.reshape(nrows, groups * 128), F32), aug


def _trunk(x, norm_mix_w, w_in, q_norm_w, k_norm_w, lam, attn_out_norm_w, lb_fwd, lb_bwd,
           rec_out_norm_w, w_out, norm_ffn_w, w_up, conv_w, conv_b, w_down):
    b, l, d = x.shape
    n = b * l
    x2 = x.reshape(n, d)

    seg = np.kron(np.eye(GROUP_W // ATTN_QKDIM, dtype=np.float32),
                  np.full((ATTN_QKDIM, ATTN_QKDIM), 1.0 / ATTN_QKDIM, np.float32))
    reps = GROUP_W // ATTN_QKDIM
    lbf = jnp.cumsum(jax.nn.softmax(lb_fwd.astype(F32), axis=0), axis=0)[0][None]
    lbb = jnp.cumsum(jax.nn.softmax(lb_bwd.astype(F32), axis=0), axis=0)[0][None]
    cs, kaug, aug = _alibi_operands()
    qt, kp, vt, rq, kf, kb, lf, lb, ri, sg = _inproj(
        x2, norm_mix_w[None], w_in.astype(BF16), jnp.asarray(seg, BF16),
        jnp.tile(q_norm_w, reps)[None], jnp.tile(k_norm_w, reps)[None], lbf, lbb, kaug, b)

    wcol = jnp.broadcast_to(attn_out_norm_w.astype(F32)[:, None], (ATTN_VDIM, ATTN_BQ))
    score_bound = (1.02 * ATTN_QKDIM ** 0.5 * LOG2E) * (jnp.max(jnp.abs(q_norm_w))
                                                         * jnp.max(jnp.abs(k_norm_w)))
    attn_args = (cs, lam.reshape(1), qt, kp.reshape(b, l, kp.shape[-1]), vt, aug, wcol)
    ao = lax.cond(2.0 * score_bound <= ATTN_FIXED_SHIFT_RANGE,
                  functools.partial(_attn, online=False),
                  functools.partial(_attn, online=True), *attn_args)

    shape3 = (b, l, GROUP_W)
    decay_log2 = -GLA_FACTORED_DIAG * jnp.log2(jnp.minimum(jnp.min(lbf), jnp.min(lbb)))
    gla_args = [t.reshape(shape3) for t in (rq, kf, kb, ri, lf, lb)]
    o_f, o_b = lax.cond(decay_log2 <= GLA_FACTORED_RANGE,
                        functools.partial(_gla, diag=GLA_FACTORED_DIAG),
                        functools.partial(_gla, diag=8), *gla_args)

    y = _mix_ffn(x2, ao.reshape(n, GROUP_W), o_f.reshape(n, GROUP_W), o_b.reshape(n, GROUP_W),
                 sg, jnp.tile(rec_out_norm_w, REC_HEADS)[None], w_out.astype(BF16),
                 norm_ffn_w[None], w_up.astype(BF16), conv_w, conv_b[None], w_down.astype(BF16), l)
    return y.reshape(b, l, d)


def kernel(x_prompt, x_sample, norm_mix_w, w_in, q_norm_w, k_norm_w, lambda_q1, lambda_k1,
           lambda_q2, lambda_k2, attn_out_norm_w, lb_fwd, lb_bwd, rec_out_norm_w, w_out,
           norm_ffn_w, w_up, conv_w, conv_b, w_down):
    assert norm_mix_w.shape[0] == 1, "single-layer trunk"
    lam = (jnp.exp(jnp.sum(lambda_q1[0].astype(F32) * lambda_k1[0].astype(F32)))
           - jnp.exp(jnp.sum(lambda_q2[0].astype(F32) * lambda_k2[0].astype(F32))) + LAM_INIT)
    params = (norm_mix_w[0], w_in[0], q_norm_w[0], k_norm_w[0], lam, attn_out_norm_w[0],
              lb_fwd, lb_bwd, rec_out_norm_w[0], w_out[0], norm_ffn_w[0], w_up[0], conv_w[0],
              conv_b[0], w_down[0])
    return (_trunk(x_prompt, *params), _trunk(x_sample, *params))
```

```python
import functools
import math

import numpy as np
import jax
import jax.numpy as jnp
from jax import lax
from jax.experimental import pallas as pl
from jax.experimental.pallas import tpu as pltpu

F32 = jnp.float32
BF16 = jnp.bfloat16

NORM_EPS = 1e-6
ATTN_HEADS = 4
ATTN_QKDIM = 64
ATTN_VDIM = 128
REC_HEADS = 4
REC_DIM = 128
GROUP_W = 512
LOG2E = 1.4426950408889634
LAM_INIT = 0.8 - 0.6 * math.exp(-0.3 * 0)

V7X_VMEM_LIMIT = 56 * 1024 * 1024

TOKEN_TILE = 512
ATTN_BQ = 256
ATTN_BK = 256
ATTN_QBLOCKS = 2
ATTN_LOOKAHEAD = 4
ATTN_FIXED_SHIFT_RANGE = 64.0
ATTN_VROWS = ATTN_VDIM + 16
GLA_CHUNK = 128
GLA_FACTORED_DIAG = 16
GLA_FACTORED_RANGE = 96.0
GLA_TILE = 512
FFN_FT = 256
FFN_TILES_PER_ITER = 2
HALO = 16


def _cparams(sem):
    return pltpu.CompilerParams(dimension_semantics=sem, vmem_limit_bytes=V7X_VMEM_LIMIT)


def _const_spec(shape):
    zeros = (0,) * len(shape)
    return pl.BlockSpec(shape, lambda *_: zeros)


def _inproj_body(x_ref, nw_ref, w_ref, seg_ref, qw_ref, kw_ref, lbf_ref, lbb_ref,
                 kaug_ref, q_ref, k_ref, v_ref, rq_ref, kf_ref, kb_ref, lf_ref, lb_ref, ri_ref,
                 sg_ref):
    x = x_ref[...]
    ms = jnp.mean(x * x, axis=-1, keepdims=True)
    h = (x * lax.rsqrt(ms + NORM_EPS) * nw_ref[...]).astype(BF16)

    def proj(g):
        return jnp.dot(h, w_ref[:, g * GROUP_W:(g + 1) * GROUP_W], preferred_element_type=F32)

    def head_norm(p, w):
        m = jnp.dot((p * p).astype(BF16), seg_ref[...], preferred_element_type=F32)
        return p * lax.rsqrt(m + NORM_EPS) * w

    def sigmoid(p):
        return 1.0 / (1.0 + jnp.exp(-p))

    def put_q(p):
        qn = head_norm(p, qw_ref[...]) * (ATTN_QKDIM ** -0.5 * LOG2E)
        q_ref[...] = qn.T.astype(BF16)

    def put_k(p):
        kn = head_norm(p, kw_ref[...])
        low = lax.broadcasted_iota(jnp.int32, (kn.shape[0], 128), 1) < ATTN_QKDIM
        for t in range(GROUP_W // 128):
            pair = kn[:, t * 128:(t + 1) * 128]
            for half, src in ((0, pair), (1, pltpu.roll(pair, ATTN_QKDIM, axis=1))):
                cols = slice((2 * t + half) * 128, (2 * t + half + 1) * 128)
                k_ref[:, cols] = jnp.where(low, src, kaug_ref[:, cols]).astype(BF16)

    def put_v(p):
        vt = p.T
        tail = (lax.broadcasted_iota(jnp.int32, (ATTN_VROWS - ATTN_VDIM, vt.shape[-1]), 0) == 0)
        for hd in range(ATTN_HEADS):
            v_ref[hd, :ATTN_VDIM, :] = vt[hd * ATTN_VDIM:(hd + 1) * ATTN_VDIM].astype(BF16)
            v_ref[hd, ATTN_VDIM:, :] = tail.astype(BF16)

    def put_silu(ref):
        def put(p):
            ref[...] = (p * sigmoid(p)).astype(BF16)
        return put

    def put_gate(lb_in, k_out, l_out):
        def put(p):
            lbv = lb_in[...]
            f = lbv + (1.0 - lbv) * sigmoid(p)
            k_out[...] = (1.0 - f).astype(BF16)
            l_out[...] = jnp.log2(f)
        return put

    def put_ri(p):
        ri_ref[...] = p.astype(BF16)

    epilogues = (put_q, put_k, put_v, put_silu(rq_ref), put_gate(lbf_ref, kf_ref, lf_ref),
                 put_gate(lbb_ref, kb_ref, lb_ref), put_ri, put_silu(sg_ref))
    ahead = proj(0)
    for g, epilogue in enumerate(epilogues):
        p = ahead
        if g + 1 < len(epilogues):
            ahead = proj(g + 1)
        epilogue(p)


def _inproj(x2, nw, w_in, seg, qw, kw, lbf, lbb, kaug, batch):
    n, d = x2.shape
    tm = TOKEN_TILE
    l = n // batch
    tps = l // tm
    aug_tiles = kaug.shape[0] // tm
    row = lambda i: (i, 0)
    out_bf = jax.ShapeDtypeStruct((n, GROUP_W), BF16)
    out_f = jax.ShapeDtypeStruct((n, GROUP_W), F32)
    ospec = pl.BlockSpec((tm, GROUP_W), row)
    kw_cols = kaug.shape[1]
    return pl.pallas_call(
        _inproj_body,
        grid=(n // tm,),
        in_specs=[pl.BlockSpec((tm, d), row), _const_spec((1, d)), _const_spec(w_in.shape),
                  _const_spec(seg.shape)] + [_const_spec((1, GROUP_W))] * 4
                 + [pl.BlockSpec((tm, kw_cols), lambda i: (i % aug_tiles, 0))],
        out_specs=[pl.BlockSpec((None, GROUP_W, tm), lambda i: (i // tps, 0, i % tps)),
                   pl.BlockSpec((tm, kw_cols), row),
                   pl.BlockSpec((None, ATTN_HEADS, ATTN_VROWS, tm),
                                lambda i: (i // tps, 0, 0, i % tps))] + [ospec] * 7,
        out_shape=[jax.ShapeDtypeStruct((batch, GROUP_W, l), BF16),
                   jax.ShapeDtypeStruct((n, kw_cols), BF16),
                   jax.ShapeDtypeStruct((batch, ATTN_HEADS, ATTN_VROWS, l), BF16),
                   out_bf, out_bf, out_bf, out_f, out_f, out_bf, out_bf],
        compiler_params=_cparams(("parallel",)),
    )(x2, nw, w_in, seg, qw, kw, lbf, lbb, kaug)


def _attn_body(c_ref, lam_ref, qt_ref, k_ref, vt_ref, aug_ref, wcol_ref, o_ref, m_ref, acc_ref,
               *, bq, bk, nkv, nqb, online):
    c = c_ref[pl.program_id(1)]
    lam = lam_ref[0]
    rows = lax.broadcasted_iota(jnp.int32, (bk, bq), 0)

    if online:
        m_ref[...] = jnp.full(m_ref.shape, -jnp.inf, F32)
        acc_ref[...] = jnp.zeros(acc_ref.shape, F32)

    def blocks_of(qi):
        t0 = (pl.program_id(2) * nqb + qi) * bq
        jd = lax.div(t0, bk)
        tg = t0 + lax.broadcasted_iota(jnp.int32, (1, bq), 1)
        dist = jnp.abs((tg - jd * bk) - rows).astype(F32)
        blocks = [(qi, tg, jd, jnp.zeros((ATTN_QKDIM, bq), BF16), 0.0, dist * (-c))]
        for idx in range(nkv - 1):
            j = idx + (idx >= jd).astype(jnp.int32)
            sgn = jnp.where(j < jd, 1.0, -1.0)
            blocks.append((qi, tg, j, (aug_ref[...] * sgn).astype(BF16), -sgn * c, None))
        return blocks

    def scores(blk, mp):
        qi, _, j, aug, _, bias = blk
        s0 = pl.multiple_of(j * bk, bk)
        q = qt_ref[mp * ATTN_QKDIM:(mp + 1) * ATTN_QKDIM, qi * bq:(qi + 1) * bq]
        st = jnp.dot(k_ref[pl.ds(s0, bk), mp * 128:(mp + 1) * 128],
                     jnp.concatenate([q, aug], axis=0), preferred_element_type=F32)
        return st if bias is None else st + bias

    def fold(blk, mp, st):
        qi, tg, j, _, wcoef, bias = blk
        s0 = pl.multiple_of(j * bk, bk)
        w = wcoef * (tg - s0).astype(F32)
        vt = vt_ref[:, pl.ds(s0, bk)]
        if online:
            m_old = m_ref[qi, mp]
            m_new = jnp.maximum(m_old, jnp.max(st, axis=0, keepdims=True) + w)
            alpha = jnp.exp2(m_old - m_new)
            p = jnp.exp2((st - (m_new - w)).astype(BF16))
            acc_ref[qi, mp] = (acc_ref[qi, mp] * alpha
                               + jnp.dot(vt, p, preferred_element_type=F32))
            m_ref[qi, mp] = m_new
        elif bias is not None:
            shift = jnp.max(st, axis=0, keepdims=True)
            m_ref[qi, mp] = shift
            p = jnp.exp2((st - shift).astype(BF16))
            acc_ref[qi, mp] = jnp.dot(vt, p, preferred_element_type=F32)
        else:
            p = jnp.exp2((st - (m_ref[qi, mp] - w)).astype(BF16))
            acc_ref[qi, mp] += jnp.dot(vt, p, preferred_element_type=F32)

    def finish(qi):
        a1 = acc_ref[qi, 0]
        a2 = acc_ref[qi, 1]
        o1 = a1[:ATTN_VDIM] * (1.0 / a1[ATTN_VDIM:ATTN_VDIM + 1])
        o2 = a2[:ATTN_VDIM] * (1.0 / a2[ATTN_VDIM:ATTN_VDIM + 1])
        o = o1 - lam * o2
        ms = jnp.mean(o * o, axis=0, keepdims=True)
        y = o * lax.rsqrt(ms + NORM_EPS) * wcol_ref[...] * (1.0 - LAM_INIT)
        o_ref[qi * bq:(qi + 1) * bq, :] = y.T.astype(BF16)

    units = [(blk, mp) for qi in range(nqb) for blk in blocks_of(qi) for mp in range(2)]
    pending = [scores(*u) for u in units[:ATTN_LOOKAHEAD]]
    for n, (blk, mp) in enumerate(units):
        if n + ATTN_LOOKAHEAD < len(units):
            pending.append(scores(*units[n + ATTN_LOOKAHEAD]))
        fold(blk, mp, pending.pop(0))
        if n + 1 == len(units) or units[n + 1][0][0] != blk[0]:
            finish(blk[0])


def _attn(cs, lam, qt, kp, vt, aug, wcol, online):
    b, h, _, l = vt.shape
    bq, bk, nqb = ATTN_BQ, ATTN_BK, ATTN_QBLOCKS
    assert bk % bq == 0 and l % bk == 0 and l % (bq * nqb) == 0
    body = functools.partial(_attn_body, bq=bq, bk=bk, nkv=l // bk, nqb=nqb, online=online)
    smem = pl.BlockSpec(memory_space=pltpu.SMEM)
    return pl.pallas_call(
        body,
        grid=(b, h, l // (bq * nqb)),
        in_specs=[smem, smem,
                  pl.BlockSpec((None, 2 * ATTN_QKDIM, nqb * bq), lambda bi, hi, i: (bi, hi, i)),
                  pl.BlockSpec((None, l, 256), lambda bi, hi, i: (bi, 0, hi)),
                  pl.BlockSpec((None, None, ATTN_VROWS, l), lambda bi, hi, i: (bi, hi, 0, 0)),
                  pl.BlockSpec((None, ATTN_QKDIM, bq), lambda bi, hi, i: (hi, 0, 0)),
                  _const_spec((ATTN_VDIM, bq))],
        out_specs=pl.BlockSpec((None, nqb * bq, ATTN_VDIM), lambda bi, hi, i: (bi, i, hi)),
        out_shape=jax.ShapeDtypeStruct((b, l, h * ATTN_VDIM), BF16),
        scratch_shapes=[pltpu.VMEM((nqb, 2, 1, bq), F32),
                        pltpu.VMEM((nqb, 2, ATTN_VROWS, bq), F32)],
        compiler_params=_cparams(("parallel", "parallel", "arbitrary")),
    )(cs, lam, qt, kp, vt, aug, wcol)


def _gla_levels(c, diag):
    lv, half = [], c // 2
    while half >= diag:
        lv.append(half)
        half //= 2
    return lv


def _gla_constants(c, diag):
    r = np.arange(c)[:, None]
    u = np.arange(c)[None, :]
    mats = [(u <= r), (u > r)]
    bmasks = []
    for half in _gla_levels(c, diag):
        blk = r // (2 * half)
        upper = (r % (2 * half)) >= half
        mid = blk * 2 * half + half - 1
        mats.append(np.where(upper, (u > mid) & (u <= r), (u > r) & (u <= mid)))
        t, s = r, u
        bmasks.append(((t // (2 * half)) == (s // (2 * half)))
                      & ((t % (2 * half)) >= half) & ((s % (2 * half)) < half))
    if diag > 8:
        mats.append((u // diag == r // diag) & (u <= r))
    dmask = ((r // diag) == (u // diag)) & (u <= r)
    fwd_m = np.concatenate([m.astype(np.float32) for m in mats], axis=0)
    fwd_b = np.stack([m.astype(np.float32) for m in bmasks])
    fwd_d = dmask.astype(np.float32)
    flip = lambda m: m[::-1, ::-1]
    bwd_m = np.concatenate([flip(m.astype(np.float32)) for m in mats], axis=0)
    bwd_b = np.stack([flip(m.astype(np.float32)) for m in bmasks])
    bwd_d = flip(fwd_d)
    fwd_m, bwd_m = (np.concatenate([m, m], axis=1) for m in (fwd_m, bwd_m))
    return (jnp.asarray(np.stack([fwd_m, bwd_m]), BF16),
            jnp.asarray(np.stack([fwd_b, bwd_b]), F32),
            jnp.asarray(np.stack([fwd_d, bwd_d]), F32))


def _gla_body(qf_ref, kf_ref, vf_ref, lf_ref, qb_ref, kb_ref, vb_ref, lb_ref,
              mst_ref, bm_ref, dm_ref, of_ref, ob_ref, s_ref, *, c, nch, diag):
    nlev = len(_gla_levels(c, diag))
    nt = (((1,), (1,)), ((), ()))

    @pl.when(pl.program_id(1) == 0)
    def _():
        s_ref[...] = jnp.zeros(s_ref.shape, F32)

    lane = lax.broadcasted_iota(jnp.int32, (8, c), 1)
    dirs = ((qf_ref, kf_ref, vf_ref, lf_ref, of_ref), (qb_ref, kb_ref, vb_ref, lb_ref, ob_ref))

    def chunk(ci, carry):
        exps, rows = [], []
        for d, (_, _, _, l_ref, _) in enumerate(dirs):
            cc = ci if d == 0 else nch - 1 - ci
            r0 = pl.multiple_of(cc * c, c)
            lg = l_ref[pl.ds(r0, c), :]
            h1 = lg.astype(BF16)
            h2 = (lg - h1.astype(F32)).astype(BF16)
            exps.append(jnp.dot(mst_ref[d], jnp.concatenate([h1, h2], axis=0),
                                preferred_element_type=F32))
            rows.append(r0)

        units = []
        for hd in range(REC_HEADS):
            cols = slice(hd * REC_DIM, (hd + 1) * REC_DIM)
            for d, (q_ref, k_ref, v_ref, _, _) in enumerate(dirs):
                ex, r0 = exps[d], rows[d]
                g = ex[0:c, cols]
                q = q_ref[pl.ds(r0, c), cols].astype(F32)
                k = k_ref[pl.ds(r0, c), cols].astype(F32)
                st = s_ref[d, hd]
                o = lax.dot_general((q * jnp.exp2(g)).astype(BF16), st.astype(BF16), nt,
                                    preferred_element_type=F32)
                a = jnp.zeros((c, c), F32)
                for li in range(nlev):
                    eh = jnp.exp2(ex[(2 + li) * c:(3 + li) * c, cols])
                    pr = lax.dot_general((q * eh).astype(BF16), (k * eh).astype(BF16), nt,
                                         preferred_element_type=F32)
                    a = a + bm_ref[d, li] * pr
                if diag > 8:
                    ed = ex[(2 + nlev) * c:(3 + nlev) * c, cols]
                    pr = lax.dot_general((q * jnp.exp2(ed)).astype(BF16),
                                         (k * jnp.exp2(-ed)).astype(BF16), nt,
                                         preferred_element_type=F32)
                    a = a + jnp.where(dm_ref[d] > 0.0, pr, 0.0)
                units.append((hd, d, cols, g, q, k, st, o, a))

        diags = []
        for hd, d, cols, g, q, k, st, o, a in units:
            if diag > 8:
                diags.append(None)
                continue
            strips = []
            for blk in range(c // 8):
                rs = slice(8 * blk, 8 * blk + 8)
                gb, qb, kb = g[rs], q[rs], k[rs]
                strip = jnp.zeros((8, c), F32)
                for s in range(8):
                    e = jnp.exp2(gb - gb[s:s + 1])
                    col = jnp.sum(qb * (kb[s:s + 1] * e), axis=-1, keepdims=True)
                    strip = jnp.where(lane == 8 * blk + s, col, strip)
                strips.append(strip)
            diags.append(jnp.where(dm_ref[d] > 0.0, jnp.concatenate(strips, axis=0), 0.0))

        for (hd, d, cols, g, q, k, st, o, a), dg in zip(units, diags):
            v_ref, o_ref = dirs[d][2], dirs[d][4]
            ex, r0 = exps[d], rows[d]
            v = v_ref[pl.ds(r0, c), cols]
            if dg is not None:
                a = a + dg
            o = o + jnp.dot(a.astype(BF16), v, preferred_element_type=F32)
            o_ref[pl.ds(r0, c), cols] = o.astype(BF16)
            kt = (k * jnp.exp2(ex[c:2 * c, cols])).astype(BF16)
            g_all = g[c - 1:c] if d == 0 else g[0:1]
            s_ref[d, hd] = st * jnp.exp2(g_all) + jnp.dot(
                v.astype(F32).T.astype(BF16), kt, preferred_element_type=F32)
        return carry

    lax.fori_loop(0, nch, chunk, 0, unroll=2)


def _gla(rq, kf, kb, ri, lf, lb, diag):
    b, l, w = rq.shape
    c, tc = GLA_CHUNK, GLA_TILE
    n = l // tc
    mst, bm, dm = _gla_constants(c, diag)
    fwd = pl.BlockSpec((None, tc, w), lambda bi, i: (bi, i, 0))
    bwd = pl.BlockSpec((None, tc, w), lambda bi, i: (bi, n - 1 - i, 0))
    out = jax.ShapeDtypeStruct((b, l, w), BF16)
    return pl.pallas_call(
        functools.partial(_gla_body, c=c, nch=tc // c, diag=diag),
        grid=(b, n),
        in_specs=[fwd, fwd, fwd, fwd, bwd, bwd, bwd, bwd,
                  _const_spec(mst.shape), _const_spec(bm.shape), _const_spec(dm.shape)],
        out_specs=[fwd, bwd],
        out_shape=[out, out],
        scratch_shapes=[pltpu.VMEM((2, REC_HEADS, REC_DIM, REC_DIM), F32)],
        compiler_params=_cparams(("parallel", "arbitrary")),
    )(rq, kf, ri, lf, rq, kb, ri, lb, mst, bm, dm)


def _mix_residual(x, ao, o_f, o_b, sg, rw_ref, wo_ref):
    ro = o_f.astype(F32) + o_b.astype(F32)
    parts = []
    for hd in range(REC_HEADS):
        seg = ro[:, hd * REC_DIM:(hd + 1) * REC_DIM]
        ms = jnp.mean(seg * seg, axis=-1, keepdims=True)
        parts.append(seg * lax.rsqrt(ms + NORM_EPS))
    ron = jnp.concatenate(parts, axis=-1) * rw_ref[...] * sg.astype(F32)
    aw = ao.shape[-1]
    mix = (jnp.dot(ao, wo_ref[:aw, :], preferred_element_type=F32)
           + jnp.dot(ron.astype(BF16), wo_ref[aw:, :], preferred_element_type=F32))
    return x + mix


def _mix_ffn_body(*refs, tm, ft, dff, tiles_per_seq):
    rows = [refs[3 * a:3 * a + 3] for a in range(5)]
    (rw_ref, wo_ref, fw_ref, wu_ref, cw_ref, cb_ref, wd_ref, y_ref, acc_ref, u_ref) = refs[15:]
    x, ao, o_f, o_b, sg = [jnp.concatenate([r[...] for r in trio], axis=0) for trio in rows]
    ext = tm + 2 * HALO
    x1 = _mix_residual(x, ao, o_f, o_b, sg, rw_ref, wo_ref)
    ms = jnp.mean(x1 * x1, axis=-1, keepdims=True)
    h2 = x1 * lax.rsqrt(ms + NORM_EPS) * fw_ref[...]
    pos = lax.rem(pl.program_id(0), tiles_per_seq)
    r = lax.broadcasted_iota(jnp.int32, (ext, 1), 0)
    keep = (((r >= HALO) | (pos != 0)) & ((r < HALO + tm) | (pos != tiles_per_seq - 1)))
    hx = jnp.where(keep, h2, 0.0).astype(BF16)
    acc_ref[...] = x1[HALO:HALO + tm]

    def conv(u, col):
        prev = pltpu.roll(u, 1, axis=0)[HALO:HALO + tm]
        nxt = pltpu.roll(u, ext - 1, axis=0)[HALO:HALO + tm]
        cw = cw_ref[:, pl.ds(col, ft)]
        return (cb_ref[:, pl.ds(col, ft)] + prev * cw[0:1] + u[HALO:HALO + tm] * cw[1:2]
                + nxt * cw[2:3])

    def up_dots(f, slot):
        cg = pl.multiple_of(f * ft, ft)
        cu = pl.multiple_of(dff + f * ft, ft)
        u_ref[slot, 0] = jnp.dot(hx, wu_ref[:, pl.ds(cg, ft)], preferred_element_type=F32)
        u_ref[slot, 1] = jnp.dot(hx, wu_ref[:, pl.ds(cu, ft)], preferred_element_type=F32)

    def finish(f, slot):
        cg = pl.multiple_of(f * ft, ft)
        cu = pl.multiple_of(dff + f * ft, ft)
        gate = conv(u_ref[slot, 0], cg)
        up = conv(u_ref[slot, 1], cu)
        act = (gate * (1.0 / (1.0 + jnp.exp(-gate))) * up).astype(BF16)
        acc_ref[...] += jnp.dot(act, wd_ref[pl.ds(cg, ft), :], preferred_element_type=F32)

    nf = dff // ft
    per_iter = FFN_TILES_PER_ITER

    def several(i, carry):
        for u in range(per_iter):
            f = per_iter * i + u
            up_dots(f + 1, (u + 1) % 2)
            finish(f, u % 2)
        return carry

    up_dots(0, 0)
    looped = (nf - 1) // per_iter
    lax.fori_loop(0, looped, several, 0)
    for f in range(per_iter * looped, nf):
        if f + 1 < nf:
            up_dots(f + 1, (f + 1) % 2)
        finish(f, f % 2)
    y_ref[...] = acc_ref[...]


def _mix_ffn(x2, ao, o_f, o_b, sg, rw, w_out, fw, w_up, conv_w, conv_b, w_down, seq_len):
    n, d = x2.shape
    tm, ft = TOKEN_TILE, FFN_FT
    dff = w_down.shape[0]
    hb = tm // HALO
    nhb = n // HALO

    def trio(width):
        return [pl.BlockSpec((HALO, width), lambda i: (jnp.maximum(i * hb - 1, 0), 0)),
                pl.BlockSpec((tm, width), lambda i: (i, 0)),
                pl.BlockSpec((HALO, width), lambda i: (jnp.minimum((i + 1) * hb, nhb - 1), 0))]

    body = functools.partial(_mix_ffn_body, tm=tm, ft=ft, dff=dff, tiles_per_seq=seq_len // tm)
    row_inputs = (x2, ao, o_f, o_b, sg)
    consts = (rw, w_out, fw, w_up, conv_w, conv_b, w_down)
    return pl.pallas_call(
        body,
        grid=(n // tm,),
        in_specs=[spec for a in row_inputs for spec in trio(a.shape[1])]
                 + [_const_spec(a.shape) for a in consts],
        out_specs=pl.BlockSpec((tm, d), lambda i: (i, 0)),
        out_shape=jax.ShapeDtypeStruct((n, d), F32),
        scratch_shapes=[pltpu.VMEM((tm, d), F32), pltpu.VMEM((2, 2, tm + 2 * HALO, ft), F32)],
        compiler_params=_cparams(("parallel",)),
    )(*[a for a in row_inputs for _ in range(3)], *consts)


def _bf16_split3(x):
    x = x.astype(F32)
    a = x.astype(BF16)
    r = x - a.astype(F32)
    b = r.astype(BF16)
    c = (r - b.astype(F32)).astype(BF16)
    return a, b, c


def _alibi_operands():
    start = 2.0 ** (-8.0 / ATTN_HEADS)
    slopes = np.array([start ** (i + 1) for i in range(ATTN_HEADS)], np.float32)
    cs = jnp.asarray(slopes * np.float32(LOG2E), F32)
    groups = GROUP_W // ATTN_QKDIM
    nrows = max(ATTN_BK, TOKEN_TILE)
    pos = np.arange(nrows) % ATTN_BK
    kaug = np.zeros((nrows, groups, 128), np.float32)
    kaug[:, :, ATTN_QKDIM:ATTN_QKDIM + 3] = (pos - pos % 16)[:, None, None]
    kaug[:, :, ATTN_QKDIM + 3:ATTN_QKDIM + 6] = (pos % 16)[:, None, None]
    c1, c2, c3 = _bf16_split3(cs)
    six = jnp.stack([c1, c2, c3, c1, c2, c3], axis=1).astype(F32)
    rows = jnp.zeros((ATTN_HEADS, ATTN_QKDIM), F32).at[:, :6].set(six)
    aug = jnp.broadcast_to(rows[..., None], rows.shape + (ATTN_BQ,))
    return cs, jnp.asarray(kaug.reshape(nrows, groups * 128), F32), aug


def _trunk(x, norm_mix_w, w_in, q_norm_w, k_norm_w, lam, attn_out_norm_w, lb_fwd, lb_bwd,
           rec_out_norm_w, w_out, norm_ffn_w, w_up, conv_w, conv_b, w_down):
    b, l, d = x.shape
    n = b * l
    x2 = x.reshape(n, d)

    seg = np.kron(np.eye(GROUP_W // ATTN_QKDIM, dtype=np.float32),
                  np.full((ATTN_QKDIM, ATTN_QKDIM), 1.0 / ATTN_QKDIM, np.float32))
    reps = GROUP_W // ATTN_QKDIM
    lbf = jnp.cumsum(jax.nn.softmax(lb_fwd.astype(F32), axis=0), axis=0)[0][None]
    lbb = jnp.cumsum(jax.nn.softmax(lb_bwd.astype(F32), axis=0), axis=0)[0][None]
    cs, kaug, aug = _alibi_operands()
    qt, kp, vt, rq, kf, kb, lf, lb, ri, sg = _inproj(
        x2, norm_mix_w[None], w_in.astype(BF16), jnp.asarray(seg, BF16),
        jnp.tile(q_norm_w, reps)[None], jnp.tile(k_norm_w, reps)[None], lbf, lbb, kaug, b)

    wcol = jnp.broadcast_to(attn_out_norm_w.astype(F32)[:, None], (ATTN_VDIM, ATTN_BQ))
    score_bound = (1.02 * ATTN_QKDIM ** 0.5 * LOG2E) * (jnp.max(jnp.abs(q_norm_w))
                                                         * jnp.max(jnp.abs(k_norm_w)))
    attn_args = (cs, lam.reshape(1), qt, kp.reshape(b, l, kp.shape[-1]), vt, aug, wcol)
    ao = lax.cond(2.0 * score_bound <= ATTN_FIXED_SHIFT_RANGE,
                  functools.partial(_attn, online=False),
                  functools.partial(_attn, online=True), *attn_args)

    shape3 = (b, l, GROUP_W)
    decay_log2 = -GLA_FACTORED_DIAG * jnp.log2(jnp.minimum(jnp.min(lbf), jnp.min(lbb)))
    gla_args = [t.reshape(shape3) for t in (rq, kf, kb, ri, lf, lb)]
    o_f, o_b = lax.cond(decay_log2 <= GLA_FACTORED_RANGE,
                        functools.partial(_gla, diag=GLA_FACTORED_DIAG),
                        functools.partial(_gla, diag=8), *gla_args)

    y = _mix_ffn(x2, ao.reshape(n, GROUP_W), o_f.reshape(n, GROUP_W), o_b.reshape(n, GROUP_W),
                 sg, jnp.tile(rec_out_norm_w, REC_HEADS)[None], w_out.astype(BF16),
                 norm_ffn_w[None], w_up.astype(BF16), conv_w, conv_b[None], w_down.astype(BF16), l)
    return y.reshape(b, l, d)


def kernel(x_prompt, x_sample, norm_mix_w, w_in, q_norm_w, k_norm_w, lambda_q1, lambda_k1,
           lambda_q2, lambda_k2, attn_out_norm_w, lb_fwd, lb_bwd, rec_out_norm_w, w_out,
           norm_ffn_w, w_up, conv_w, conv_b, w_down):
    assert norm_mix_w.shape[0] == 1, "single-layer trunk"
    lam = (jnp.exp(jnp.sum(lambda_q1[0].astype(F32) * lambda_k1[0].astype(F32)))
           - jnp.exp(jnp.sum(lambda_q2[0].astype(F32) * lambda_k2[0].astype(F32))) + LAM_INIT)
    params = (norm_mix_w[0], w_in[0], q_norm_w[0], k_norm_w[0], lam, attn_out_norm_w[0],
              lb_fwd, lb_bwd, rec_out_norm_w[0], w_out[0], norm_ffn_w[0], w_up[0], conv_w[0],
              conv_b[0], w_down[0])
    return (_trunk(x_prompt, *params), _trunk(x_sample, *params))
```

```python
import functools
import math

import numpy as np
import jax
import jax.numpy as jnp
from jax import lax
from jax.experimental import pallas as pl
from jax.experimental.pallas import tpu as pltpu

F32 = jnp.float32
BF16 = jnp.bfloat16

NORM_EPS = 1e-6
ATTN_HEADS = 4
ATTN_QKDIM = 64
ATTN_VDIM = 128
REC_HEADS = 4
REC_DIM = 128
GROUP_W = 512
LOG2E = 1.4426950408889634
LAM_INIT = 0.8 - 0.6 * math.exp(-0.3 * 0)

V7X_VMEM_LIMIT = 56 * 1024 * 1024

TOKEN_TILE = 512
INPROJ_LOOKAHEAD = 2
ATTN_BQ = 256
ATTN_BK = 256
ATTN_QBLOCKS = 2
ATTN_LOOKAHEAD = 4
ATTN_FIXED_SHIFT_RANGE = 64.0
ATTN_VROWS = ATTN_VDIM + 16
GLA_CHUNK = 128
GLA_FACTORED_DIAG = 16
GLA_FACTORED_RANGE = 96.0
GLA_TILE = 512
FFN_FT = 256
FFN_LOOKAHEAD = 1
HALO = 16


def _cparams(sem):
    return pltpu.CompilerParams(dimension_semantics=sem, vmem_limit_bytes=V7X_VMEM_LIMIT)


def _sigmoid(x):
    return 0.5 * jnp.tanh(0.5 * x) + 0.5


def _const_spec(shape):
    zeros = (0,) * len(shape)
    return pl.BlockSpec(shape, lambda *_: zeros)


def _inproj_body(x_ref, nw_ref, w_ref, seg_ref, qw_ref, kw_ref, lbf_ref, lbb_ref,
                 kaug_ref, q_ref, k_ref, v_ref, rq_ref, kf_ref, kb_ref, lf_ref, lb_ref, ri_ref,
                 sg_ref):
    x = x_ref[...]
    ms = jnp.mean(x * x, axis=-1, keepdims=True)
    h = (x * lax.rsqrt(ms + NORM_EPS) * nw_ref[...]).astype(BF16)

    def proj(g):
        return jnp.dot(h, w_ref[:, g * GROUP_W:(g + 1) * GROUP_W], preferred_element_type=F32)

    def head_norm(p, w):
        m = jnp.dot((p * p).astype(BF16), seg_ref[...], preferred_element_type=F32)
        return p * lax.rsqrt(m + NORM_EPS) * w


    def put_q(p):
        qn = head_norm(p, qw_ref[...]) * (ATTN_QKDIM ** -0.5 * LOG2E)
        q_ref[...] = qn.T.astype(BF16)

    def put_k(p):
        kn = head_norm(p, kw_ref[...])
        low = lax.broadcasted_iota(jnp.int32, (kn.shape[0], 128), 1) < ATTN_QKDIM
        for t in range(GROUP_W // 128):
            pair = kn[:, t * 128:(t + 1) * 128]
            for half, src in ((0, pair), (1, pltpu.roll(pair, ATTN_QKDIM, axis=1))):
                cols = slice((2 * t + half) * 128, (2 * t + half + 1) * 128)
                k_ref[:, cols] = jnp.where(low, src, kaug_ref[:, cols]).astype(BF16)

    def put_v(p):
        vt = p.T
        tail = (lax.broadcasted_iota(jnp.int32, (ATTN_VROWS - ATTN_VDIM, vt.shape[-1]), 0) == 0)
        for hd in range(ATTN_HEADS):
            v_ref[hd, :ATTN_VDIM, :] = vt[hd * ATTN_VDIM:(hd + 1) * ATTN_VDIM].astype(BF16)
            v_ref[hd, ATTN_VDIM:, :] = tail.astype(BF16)

    def put_silu(ref):
        def put(p):
            ref[...] = (p * _sigmoid(p)).astype(BF16)
        return put

    def put_gate(lb_in, k_out, l_out):
        def put(p):
            lbv = lb_in[...]
            f = lbv + (1.0 - lbv) * _sigmoid(p)
            k_out[...] = (1.0 - f).astype(BF16)
            l_out[...] = jnp.log2(f)
        return put

    def put_ri(p):
        ri_ref[...] = p.astype(BF16)

    epilogues = (put_q, put_k, put_v, put_silu(rq_ref), put_gate(lbf_ref, kf_ref, lf_ref),
                 put_gate(lbb_ref, kb_ref, lb_ref), put_ri, put_silu(sg_ref))
    ahead = [proj(g) for g in range(INPROJ_LOOKAHEAD)]
    for g, epilogue in enumerate(epilogues):
        if g + INPROJ_LOOKAHEAD < len(epilogues):
            ahead.append(proj(g + INPROJ_LOOKAHEAD))
        epilogue(ahead.pop(0))


def _inproj(x2, nw, w_in, seg, qw, kw, lbf, lbb, kaug, batch):
    n, d = x2.shape
    tm = TOKEN_TILE
    l = n // batch
    tps = l // tm
    aug_tiles = kaug.shape[0] // tm
    row = lambda i: (i, 0)
    out_bf = jax.ShapeDtypeStruct((n, GROUP_W), BF16)
    out_f = jax.ShapeDtypeStruct((n, GROUP_W), F32)
    ospec = pl.BlockSpec((tm, GROUP_W), row)
    kw_cols = kaug.shape[1]
    return pl.pallas_call(
        _inproj_body,
        grid=(n // tm,),
        in_specs=[pl.BlockSpec((tm, d), row), _const_spec((1, d)), _const_spec(w_in.shape),
                  _const_spec(seg.shape)] + [_const_spec((1, GROUP_W))] * 4
                 + [pl.BlockSpec((tm, kw_cols), lambda i: (i % aug_tiles, 0))],
        out_specs=[pl.BlockSpec((None, GROUP_W, tm), lambda i: (i // tps, 0, i % tps)),
                   pl.BlockSpec((tm, kw_cols), row),
                   pl.BlockSpec((None, ATTN_HEADS, ATTN_VROWS, tm),
                                lambda i: (i // tps, 0, 0, i % tps))] + [ospec] * 7,
        out_shape=[jax.ShapeDtypeStruct((batch, GROUP_W, l), BF16),
                   jax.ShapeDtypeStruct((n, kw_cols), BF16),
                   jax.ShapeDtypeStruct((batch, ATTN_HEADS, ATTN_VROWS, l), BF16),
                   out_bf, out_bf, out_bf, out_f, out_f, out_bf, out_bf],
        compiler_params=_cparams(("parallel",)),
    )(x2, nw, w_in, seg, qw, kw, lbf, lbb, kaug)


def _attn_body(c_ref, lam_ref, qt_ref, k_ref, vt_ref, aug_ref, wcol_ref, o_ref, m_ref, acc_ref,
               l_ref, *, bq, bk, nkv, nqb, online):
    c = c_ref[pl.program_id(1)]
    lam = lam_ref[0]
    rows = lax.broadcasted_iota(jnp.int32, (bk, bq), 0)

    if online:
        m_ref[...] = jnp.full(m_ref.shape, -jnp.inf, F32)
        acc_ref[...] = jnp.zeros(acc_ref.shape, F32)

    def blocks_of(qi):
        t0 = (pl.program_id(2) * nqb + qi) * bq
        jd = lax.div(t0, bk)
        tg = t0 + lax.broadcasted_iota(jnp.int32, (1, bq), 1)
        dist = jnp.abs((tg - jd * bk) - rows).astype(F32)
        blocks = [(qi, tg, jd, jnp.zeros((ATTN_QKDIM, bq), BF16), 0.0, dist * (-c))]
        for idx in range(nkv - 1):
            j = idx + (idx >= jd).astype(jnp.int32)
            sgn = jnp.where(j < jd, 1.0, -1.0)
            blocks.append((qi, tg, j, (aug_ref[...] * sgn).astype(BF16), -sgn * c, None))
        return blocks

    def scores(blk, mp):
        qi, _, j, aug, _, bias = blk
        s0 = pl.multiple_of(j * bk, bk)
        q = qt_ref[mp * ATTN_QKDIM:(mp + 1) * ATTN_QKDIM, qi * bq:(qi + 1) * bq]
        st = jnp.dot(k_ref[pl.ds(s0, bk), mp * 128:(mp + 1) * 128],
                     jnp.concatenate([q, aug], axis=0), preferred_element_type=F32)
        return st if bias is None else st + bias

    def fold(blk, mp, st):
        qi, tg, j, _, wcoef, bias = blk
        s0 = pl.multiple_of(j * bk, bk)
        w = wcoef * (tg - s0).astype(F32)
        vt = vt_ref[:, pl.ds(s0, bk)]
        if online:
            m_old = m_ref[qi, mp]
            m_new = jnp.maximum(m_old, jnp.max(st, axis=0, keepdims=True) + w)
            alpha = jnp.exp2(m_old - m_new)
            p = jnp.exp2((st - (m_new - w)).astype(BF16))
            acc_ref[qi, mp] = (acc_ref[qi, mp] * alpha
                               + jnp.dot(vt, p, preferred_element_type=F32))
            m_ref[qi, mp] = m_new
        elif bias is not None:
            shift = jnp.max(st, axis=0, keepdims=True)
            m_ref[qi, mp] = shift
            p = jnp.exp2(st - shift)
            l_ref[qi, mp] = jnp.sum(p.reshape(bk // 8, 8, bq), axis=0)
            acc_ref[qi, mp, :ATTN_VDIM] = jnp.dot(vt_ref[:ATTN_VDIM, pl.ds(s0, bk)], p.astype(BF16),
                                                  preferred_element_type=F32)
        else:
            p = jnp.exp2(st - (m_ref[qi, mp] - w))
            l_ref[qi, mp] += jnp.sum(p.reshape(bk // 8, 8, bq), axis=0)
            acc_ref[qi, mp, :ATTN_VDIM] += jnp.dot(vt_ref[:ATTN_VDIM, pl.ds(s0, bk)], p.astype(BF16),
                                                   preferred_element_type=F32)

    def finish(qi):
        outs = []
        for mp in range(2):
            a = acc_ref[qi, mp]
            if online:
                rowsum = a[ATTN_VDIM:ATTN_VDIM + 1]
            else:
                rowsum = jnp.sum(l_ref[qi, mp], axis=0, keepdims=True)
            outs.append(a[:ATTN_VDIM] * (1.0 / rowsum))
        o = outs[0] - lam * outs[1]
        ms = jnp.mean(o * o, axis=0, keepdims=True)
        y = o * lax.rsqrt(ms + NORM_EPS) * wcol_ref[...] * (1.0 - LAM_INIT)
        o_ref[qi * bq:(qi + 1) * bq, :] = y.T.astype(BF16)

    units = [(blk, mp) for qi in range(nqb) for blk in blocks_of(qi) for mp in range(2)]
    pending = [scores(*u) for u in units[:ATTN_LOOKAHEAD]]
    for n, (blk, mp) in enumerate(units):
        if n + ATTN_LOOKAHEAD < len(units):
            pending.append(scores(*units[n + ATTN_LOOKAHEAD]))
        fold(blk, mp, pending.pop(0))
        if n + 1 == len(units) or units[n + 1][0][0] != blk[0]:
            finish(blk[0])


def _attn(cs, lam, qt, kp, vt, aug, wcol, online):
    b, h, _, l = vt.shape
    bq, bk, nqb = ATTN_BQ, ATTN_BK, ATTN_QBLOCKS
    assert bk % bq == 0 and l % bk == 0 and l % (bq * nqb) == 0
    body = functools.partial(_attn_body, bq=bq, bk=bk, nkv=l // bk, nqb=nqb, online=online)
    smem = pl.BlockSpec(memory_space=pltpu.SMEM)
    return pl.pallas_call(
        body,
        grid=(b, h, l // (bq * nqb)),
        in_specs=[smem, smem,
                  pl.BlockSpec((None, 2 * ATTN_QKDIM, nqb * bq), lambda bi, hi, i: (bi, hi, i)),
                  pl.BlockSpec((None, l, 256), lambda bi, hi, i: (bi, 0, hi)),
                  pl.BlockSpec((None, None, ATTN_VROWS, l), lambda bi, hi, i: (bi, hi, 0, 0)),
                  pl.BlockSpec((None, ATTN_QKDIM, bq), lambda bi, hi, i: (hi, 0, 0)),
                  _const_spec((ATTN_VDIM, bq))],
        out_specs=pl.BlockSpec((None, nqb * bq, ATTN_VDIM), lambda bi, hi, i: (bi, i, hi)),
        out_shape=jax.ShapeDtypeStruct((b, l, h * ATTN_VDIM), BF16),
        scratch_shapes=[pltpu.VMEM((nqb, 2, 1, bq), F32),
                        pltpu.VMEM((nqb, 2, ATTN_VROWS, bq), F32),
                        pltpu.VMEM((nqb, 2, 8, bq), F32)],
        compiler_params=_cparams(("parallel", "parallel", "arbitrary")),
    )(cs, lam, qt, kp, vt, aug, wcol)


def _gla_levels(c, diag):
    lv, half = [], c // 2
    while half >= diag:
        lv.append(half)
        half //= 2
    return lv


def _gla_constants(c, diag):
    r = np.arange(c)[:, None]
    u = np.arange(c)[None, :]
    mats = [(u <= r), (u > r)]
    bmasks = []
    for half in _gla_levels(c, diag):
        blk = r // (2 * half)
        upper = (r % (2 * half)) >= half
        mid = blk * 2 * half + half - 1
        mats.append(np.where(upper, (u > mid) & (u <= r), (u > r) & (u <= mid)))
        t, s = r, u
        bmasks.append(((t // (2 * half)) == (s // (2 * half)))
                      & ((t % (2 * half)) >= half) & ((s % (2 * half)) < half))
    if diag > 8:
        mats.append((u // diag == r // diag) & (u <= r))
    dmask = ((r // diag) == (u // diag)) & (u <= r)
    fwd_m = np.concatenate([m.astype(np.float32) for m in mats], axis=0)
    fwd_b = np.stack([m.astype(np.float32) for m in bmasks])
    fwd_d = dmask.astype(np.float32)
    flip = lambda m: m[::-1, ::-1]
    bwd_m = np.concatenate([flip(m.astype(np.float32)) for m in mats], axis=0)
    bwd_b = np.stack([flip(m.astype(np.float32)) for m in bmasks])
    bwd_d = flip(fwd_d)
    fwd_m, bwd_m = (np.concatenate([m, m], axis=1) for m in (fwd_m, bwd_m))
    return (jnp.asarray(np.stack([fwd_m, bwd_m]), BF16),
            jnp.asarray(np.stack([fwd_b, bwd_b]), F32),
            jnp.asarray(np.stack([fwd_d, bwd_d]), F32))


def _gla_body(qf_ref, kf_ref, vf_ref, lf_ref, qb_ref, kb_ref, vb_ref, lb_ref,
              mst_ref, bm_ref, dm_ref, of_ref, ob_ref, s_ref, *, c, nch, diag):
    nlev = len(_gla_levels(c, diag))
    nt = (((1,), (1,)), ((), ()))

    @pl.when(pl.program_id(1) == 0)
    def _():
        s_ref[...] = jnp.zeros(s_ref.shape, F32)

    lane = lax.broadcasted_iota(jnp.int32, (8, c), 1)
    dirs = ((qf_ref, kf_ref, vf_ref, lf_ref, of_ref), (qb_ref, kb_ref, vb_ref, lb_ref, ob_ref))

    def chunk(ci, carry):
        exps, rows = [], []
        for d, (_, _, _, l_ref, _) in enumerate(dirs):
            cc = ci if d == 0 else nch - 1 - ci
            r0 = pl.multiple_of(cc * c, c)
            lg = l_ref[pl.ds(r0, c), :]
            h1 = lg.astype(BF16)
            h2 = (lg - h1.astype(F32)).astype(BF16)
            exps.append(jnp.dot(mst_ref[d], jnp.concatenate([h1, h2], axis=0),
                                preferred_element_type=F32))
            rows.append(r0)

        units = []
        for hd in range(REC_HEADS):
            cols = slice(hd * REC_DIM, (hd + 1) * REC_DIM)
            for d, (q_ref, k_ref, v_ref, _, _) in enumerate(dirs):
                ex, r0 = exps[d], rows[d]
                g = ex[0:c, cols]
                q = q_ref[pl.ds(r0, c), cols].astype(F32)
                k = k_ref[pl.ds(r0, c), cols].astype(F32)
                st = s_ref[d, hd]
                o = lax.dot_general((q * jnp.exp2(g)).astype(BF16), st.astype(BF16), nt,
                                    preferred_element_type=F32)
                a = jnp.zeros((c, c), F32)
                for li in range(nlev):
                    eh = jnp.exp2(ex[(2 + li) * c:(3 + li) * c, cols])
                    pr = lax.dot_general((q * eh).astype(BF16), (k * eh).astype(BF16), nt,
                                         preferred_element_type=F32)
                    a = a + bm_ref[d, li] * pr
                if diag > 8:
                    ed = ex[(2 + nlev) * c:(3 + nlev) * c, cols]
                    pr = lax.dot_general((q * jnp.exp2(ed)).astype(BF16),
                                         (k * jnp.exp2(-ed)).astype(BF16), nt,
                                         preferred_element_type=F32)
                    a = a + jnp.where(dm_ref[d] > 0.0, pr, 0.0)
                units.append((hd, d, cols, g, q, k, st, o, a))

        diags = []
        for hd, d, cols, g, q, k, st, o, a in units:
            if diag > 8:
                diags.append(None)
                continue
            strips = []
            for blk in range(c // 8):
                rs = slice(8 * blk, 8 * blk + 8)
                gb, qb, kb = g[rs], q[rs], k[rs]
                strip = jnp.zeros((8, c), F32)
                for s in range(8):
                    e = jnp.exp2(gb - gb[s:s + 1])
                    col = jnp.sum(qb * (kb[s:s + 1] * e), axis=-1, keepdims=True)
                    strip = jnp.where(lane == 8 * blk + s, col, strip)
                strips.append(strip)
            diags.append(jnp.where(dm_ref[d] > 0.0, jnp.concatenate(strips, axis=0), 0.0))

        for (hd, d, cols, g, q, k, st, o, a), dg in zip(units, diags):
            v_ref, o_ref = dirs[d][2], dirs[d][4]
            ex, r0 = exps[d], rows[d]
            v = v_ref[pl.ds(r0, c), cols]
            if dg is not None:
                a = a + dg
            o = o + jnp.dot(a.astype(BF16), v, preferred_element_type=F32)
            o_ref[pl.ds(r0, c), cols] = o.astype(BF16)
            kt = (k * jnp.exp2(ex[c:2 * c, cols])).astype(BF16)
            g_all = g[c - 1:c] if d == 0 else g[0:1]
            s_ref[d, hd] = st * jnp.exp2(g_all) + jnp.dot(
                v.astype(F32).T.astype(BF16), kt, preferred_element_type=F32)
        return carry

    lax.fori_loop(0, nch, chunk, 0, unroll=2)


def _gla(rq, kf, kb, ri, lf, lb, diag):
    b, l, w = rq.shape
    c, tc = GLA_CHUNK, GLA_TILE
    n = l // tc
    mst, bm, dm = _gla_constants(c, diag)
    fwd = pl.BlockSpec((None, tc, w), lambda bi, i: (bi, i, 0))
    bwd = pl.BlockSpec((None, tc, w), lambda bi, i: (bi, n - 1 - i, 0))
    out = jax.ShapeDtypeStruct((b, l, w), BF16)
    return pl.pallas_call(
        functools.partial(_gla_body, c=c, nch=tc // c, diag=diag),
        grid=(b, n),
        in_specs=[fwd, fwd, fwd, fwd, bwd, bwd, bwd, bwd,
                  _const_spec(mst.shape), _const_spec(bm.shape), _const_spec(dm.shape)],
        out_specs=[fwd, bwd],
        out_shape=[out, out],
        scratch_shapes=[pltpu.VMEM((2, REC_HEADS, REC_DIM, REC_DIM), F32)],
        compiler_params=_cparams(("parallel", "arbitrary")),
    )(rq, kf, ri, lf, rq, kb, ri, lb, mst, bm, dm)


def _mix_residual(x, ao, o_f, o_b, sg, rw_ref, wo_ref):
    ro = o_f.astype(F32) + o_b.astype(F32)
    parts = []
    for hd in range(REC_HEADS):
        seg = ro[:, hd * REC_DIM:(hd + 1) * REC_DIM]
        ms = jnp.mean(seg * seg, axis=-1, keepdims=True)
        parts.append(seg * lax.rsqrt(ms + NORM_EPS))
    ron = jnp.concatenate(parts, axis=-1) * rw_ref[...] * sg.astype(F32)
    aw = ao.shape[-1]
    mix = (jnp.dot(ao, wo_ref[:aw, :], preferred_element_type=F32)
           + jnp.dot(ron.astype(BF16), wo_ref[aw:, :], preferred_element_type=F32))
    return x + mix


def _mix_ffn_body(*refs, tm, ft, dff, tiles_per_seq):
    rows = [refs[3 * a:3 * a + 3] for a in range(5)]
    (rw_ref, wo_ref, fw_ref, wu_ref, cw_ref, cb_ref, wd_ref, y_ref, acc_ref, u_ref) = refs[15:]
    x, ao, o_f, o_b, sg = [jnp.concatenate([r[...] for r in trio], axis=0) for trio in rows]
    ext = tm + 2 * HALO
    x1 = _mix_residual(x, ao, o_f, o_b, sg, rw_ref, wo_ref)
    ms = jnp.mean(x1 * x1, axis=-1, keepdims=True)
    h2 = x1 * lax.rsqrt(ms + NORM_EPS) * fw_ref[...]
    pos = lax.rem(pl.program_id(0), tiles_per_seq)
    r = lax.broadcasted_iota(jnp.int32, (ext, 1), 0)
    keep = (((r >= HALO) | (pos != 0)) & ((r < HALO + tm) | (pos != tiles_per_seq - 1)))
    hx = jnp.where(keep, h2, 0.0).astype(BF16)
    acc_ref[...] = x1[HALO:HALO + tm]

    def conv(u, col):
        prev = pltpu.roll(u, 1, axis=0)[HALO:HALO + tm]
        nxt = pltpu.roll(u, ext - 1, axis=0)[HALO:HALO + tm]
        cw = cw_ref[:, pl.ds(col, ft)]
        return (cb_ref[:, pl.ds(col, ft)] + prev * cw[0:1] + u[HALO:HALO + tm] * cw[1:2]
                + nxt * cw[2:3])

    def up_dots(f, slot):
        cg = pl.multiple_of(f * ft, ft)
        cu = pl.multiple_of(dff + f * ft, ft)
        u_ref[slot, 0] = jnp.dot(hx, wu_ref[:, pl.ds(cg, ft)], preferred_element_type=F32)
        u_ref[slot, 1] = jnp.dot(hx, wu_ref[:, pl.ds(cu, ft)], preferred_element_type=F32)

    def finish(f, slot):
        cg = pl.multiple_of(f * ft, ft)
        cu = pl.multiple_of(dff + f * ft, ft)
        gate = conv(u_ref[slot, 0], cg)
        up = conv(u_ref[slot, 1], cu)
        act = (gate * _sigmoid(gate) * up).astype(BF16)
        acc_ref[...] += jnp.dot(act, wd_ref[pl.ds(cg, ft), :], preferred_element_type=F32)

    nf = dff // ft
    ahead = FFN_LOOKAHEAD
    slots = ahead + 1

    def rotation(i, carry):
        for u in range(slots):
            up_dots(slots * i + u + ahead, (u + ahead) % slots)
            finish(slots * i + u, u)
        return carry

    for f in range(ahead):
        up_dots(f, f)
    looped = (nf - ahead) // slots
    lax.fori_loop(0, looped, rotation, 0)
    for f in range(slots * looped, nf):
        if f + ahead < nf:
            up_dots(f + ahead, (f + ahead) % slots)
        finish(f, f % slots)
    y_ref[...] = acc_ref[...]


def _mix_ffn(x2, ao, o_f, o_b, sg, rw, w_out, fw, w_up, conv_w, conv_b, w_down, seq_len):
    n, d = x2.shape
    tm, ft = TOKEN_TILE, FFN_FT
    dff = w_down.shape[0]
    hb = tm // HALO
    nhb = n // HALO

    def trio(width):
        return [pl.BlockSpec((HALO, width), lambda i: (jnp.maximum(i * hb - 1, 0), 0)),
                pl.BlockSpec((tm, width), lambda i: (i, 0)),
                pl.BlockSpec((HALO, width), lambda i: (jnp.minimum((i + 1) * hb, nhb - 1), 0))]

    body = functools.partial(_mix_ffn_body, tm=tm, ft=ft, dff=dff, tiles_per_seq=seq_len // tm)
    row_inputs = (x2, ao, o_f, o_b, sg)
    consts = (rw, w_out, fw, w_up, conv_w, conv_b, w_down)
    return pl.pallas_call(
        body,
        grid=(n // tm,),
        in_specs=[spec for a in row_inputs for spec in trio(a.shape[1])]
                 + [_const_spec(a.shape) for a in consts],
        out_specs=pl.BlockSpec((tm, d), lambda i: (i, 0)),
        out_shape=jax.ShapeDtypeStruct((n, d), F32),
        scratch_shapes=[pltpu.VMEM((tm, d), F32),
                        pltpu.VMEM((FFN_LOOKAHEAD + 1, 2, tm + 2 * HALO, ft), F32)],
        compiler_params=_cparams(("parallel",)),
    )(*[a for a in row_inputs for _ in range(3)], *consts)


def _bf16_split3(x):
    x = x.astype(F32)
    a = x.astype(BF16)
    r = x - a.astype(F32)
    b = r.astype(BF16)
    c = (r - b.astype(F32)).astype(BF16)
    return a, b, c


def _alibi_operands():
    start = 2.0 ** (-8.0 / ATTN_HEADS)
    slopes = np.array([start ** (i + 1) for i in range(ATTN_HEADS)], np.float32)
    cs = jnp.asarray(slopes * np.float32(LOG2E), F32)
    groups = GROUP_W // ATTN_QKDIM
    nrows = max(ATTN_BK, TOKEN_TILE)
    pos = np.arange(nrows) % ATTN_BK
    kaug = np.zeros((nrows, groups, 128), np.float32)
    kaug[:, :, ATTN_QKDIM:ATTN_QKDIM + 3] = (pos - pos % 16)[:, None, None]
    kaug[:, :, ATTN_QKDIM + 3:ATTN_QKDIM + 6] = (pos % 16)[:, None, None]
    c1, c2, c3 = _bf16_split3(cs)
    six = jnp.stack([c1, c2, c3, c1, c2, c3], axis=1).astype(F32)
    rows = jnp.zeros((ATTN_HEADS, ATTN_QKDIM), F32).at[:, :6].set(six)
    aug = jnp.broadcast_to(rows[..., None], rows.shape + (ATTN_BQ,))
    return cs, jnp.asarray(kaug.reshape(nrows, groups * 128), F32), aug


def _trunk(x, norm_mix_w, w_in, q_norm_w, k_norm_w, lam, attn_out_norm_w, lb_fwd, lb_bwd,
           rec_out_norm_w, w_out, norm_ffn_w, w_up, conv_w, conv_b, w_down):
    b, l, d = x.shape
    n = b * l
    x2 = x.reshape(n, d)

    seg = np.kron(np.eye(GROUP_W // ATTN_QKDIM, dtype=np.float32),
                  np.full((ATTN_QKDIM, ATTN_QKDIM), 1.0 / ATTN_QKDIM, np.float32))
    reps = GROUP_W // ATTN_QKDIM
    lbf = jnp.cumsum(jax.nn.softmax(lb_fwd.astype(F32), axis=0), axis=0)[0][None]
    lbb = jnp.cumsum(jax.nn.softmax(lb_bwd.astype(F32), axis=0), axis=0)[0][None]
    cs, kaug, aug = _alibi_operands()
    qt, kp, vt, rq, kf, kb, lf, lb, ri, sg = _inproj(
        x2, norm_mix_w[None], w_in.astype(BF16), jnp.asarray(seg, BF16),
        jnp.tile(q_norm_w, reps)[None], jnp.tile(k_norm_w, reps)[None], lbf, lbb, kaug, b)

    wcol = jnp.broadcast_to(attn_out_norm_w.astype(F32)[:, None], (ATTN_VDIM, ATTN_BQ))
    score_bound = (1.02 * ATTN_QKDIM ** 0.5 * LOG2E) * (jnp.max(jnp.abs(q_norm_w))
                                                         * jnp.max(jnp.abs(k_norm_w)))
    attn_args = (cs, lam.reshape(1), qt, kp.reshape(b, l, kp.shape[-1]), vt, aug, wcol)
    ao = lax.cond(2.0 * score_bound <= ATTN_FIXED_SHIFT_RANGE,
                  functools.partial(_attn, online=False),
                  functools.partial(_attn, online=True), *attn_args)

    shape3 = (b, l, GROUP_W)
    decay_log2 = -GLA_FACTORED_DIAG * jnp.log2(jnp.minimum(jnp.min(lbf), jnp.min(lbb)))
    gla_args = [t.reshape(shape3) for t in (rq, kf, kb, ri, lf, lb)]
    o_f, o_b = lax.cond(decay_log2 <= GLA_FACTORED_RANGE,
                        functools.partial(_gla, diag=GLA_FACTORED_DIAG),
                        functools.partial(_gla, diag=8), *gla_args)

    y = _mix_ffn(x2, ao.reshape(n, GROUP_W), o_f.reshape(n, GROUP_W), o_b.reshape(n, GROUP_W),
                 sg, jnp.tile(rec_out_norm_w, REC_HEADS)[None], w_out.astype(BF16),
                 norm_ffn_w[None], w_up.astype(BF16), conv_w, conv_b[None], w_down.astype(BF16), l)
    return y.reshape(b, l, d)


def kernel(x_prompt, x_sample, norm_mix_w, w_in, q_norm_w, k_norm_w, lambda_q1, lambda_k1,
           lambda_q2, lambda_k2, attn_out_norm_w, lb_fwd, lb_bwd, rec_out_norm_w, w_out,
           norm_ffn_w, w_up, conv_w, conv_b, w_down):
    assert norm_mix_w.shape[0] == 1, "single-layer trunk"
    lam = (jnp.exp(jnp.sum(lambda_q1[0].astype(F32) * lambda_k1[0].astype(F32)))
           - jnp.exp(jnp.sum(lambda_q2[0].astype(F32) * lambda_k2[0].astype(F32))) + LAM_INIT)
    params = (norm_mix_w[0], w_in[0], q_norm_w[0], k_norm_w[0], lam, attn_out_norm_w[0],
              lb_fwd, lb_bwd, rec_out_norm_w[0], w_out[0], norm_ffn_w[0], w_up[0], conv_w[0],
              conv_b[0], w_down[0])
    return (_trunk(x_prompt, *params), _trunk(x_sample, *params))
```

```python
import functools
import math

import numpy as np
import jax
import jax.numpy as jnp
from jax import lax
from jax.experimental import pallas as pl
from jax.experimental.pallas import tpu as pltpu

F32 = jnp.float32
BF16 = jnp.bfloat16

NORM_EPS = 1e-6
ATTN_HEADS = 4
ATTN_QKDIM = 64
ATTN_VDIM = 128
REC_HEADS = 4
REC_DIM = 128
GROUP_W = 512
LOG2E = 1.4426950408889634
LAM_INIT = 0.8 - 0.6 * math.exp(-0.3 * 0)

V7X_VMEM_LIMIT = 56 * 1024 * 1024

TOKEN_TILE = 512
INPROJ_LOOKAHEAD = 2
ATTN_BQ = 256
ATTN_BK = 256
ATTN_QBLOCKS = 2
ATTN_LOOKAHEAD = 4
ATTN_FIXED_SHIFT_RANGE = -1.0
ATTN_VROWS = ATTN_VDIM + 16
GLA_CHUNK = 128
GLA_FACTORED_DIAG = 16
GLA_FACTORED_RANGE = -1.0
GLA_TILE = 512
FFN_FT = 256
FFN_LOOKAHEAD = 1
HALO = 16


def _cparams(sem):
    return pltpu.CompilerParams(dimension_semantics=sem, vmem_limit_bytes=V7X_VMEM_LIMIT)


def _sigmoid(x):
    return 0.5 * jnp.tanh(0.5 * x) + 0.5


def _const_spec(shape):
    zeros = (0,) * len(shape)
    return pl.BlockSpec(shape, lambda *_: zeros)


def _inproj_body(x_ref, nw_ref, w_ref, seg_ref, qw_ref, kw_ref, lbf_ref, lbb_ref,
                 kaug_ref, q_ref, k_ref, v_ref, rq_ref, kf_ref, kb_ref, lf_ref, lb_ref, ri_ref,
                 sg_ref):
    x = x_ref[...]
    ms = jnp.mean(x * x, axis=-1, keepdims=True)
    h = (x * lax.rsqrt(ms + NORM_EPS) * nw_ref[...]).astype(BF16)

    def proj(g):
        return jnp.dot(h, w_ref[:, g * GROUP_W:(g + 1) * GROUP_W], preferred_element_type=F32)

    def head_norm(p, w):
        m = jnp.dot((p * p).astype(BF16), seg_ref[...], preferred_element_type=F32)
        return p * lax.rsqrt(m + NORM_EPS) * w


    def put_q(p):
        qn = head_norm(p, qw_ref[...]) * (ATTN_QKDIM ** -0.5 * LOG2E)
        q_ref[...] = qn.T.astype(BF16)

    def put_k(p):
        kn = head_norm(p, kw_ref[...])
        low = lax.broadcasted_iota(jnp.int32, (kn.shape[0], 128), 1) < ATTN_QKDIM
        for t in range(GROUP_W // 128):
            pair = kn[:, t * 128:(t + 1) * 128]
            for half, src in ((0, pair), (1, pltpu.roll(pair, ATTN_QKDIM, axis=1))):
                cols = slice((2 * t + half) * 128, (2 * t + half + 1) * 128)
                k_ref[:, cols] = jnp.where(low, src, kaug_ref[:, cols]).astype(BF16)

    def put_v(p):
        vt = p.T
        tail = (lax.broadcasted_iota(jnp.int32, (ATTN_VROWS - ATTN_VDIM, vt.shape[-1]), 0) == 0)
        for hd in range(ATTN_HEADS):
            v_ref[hd, :ATTN_VDIM, :] = vt[hd * ATTN_VDIM:(hd + 1) * ATTN_VDIM].astype(BF16)
            v_ref[hd, ATTN_VDIM:, :] = tail.astype(BF16)

    def put_silu(ref):
        def put(p):
            ref[...] = (p * _sigmoid(p)).astype(BF16)
        return put

    def put_gate(lb_in, k_out, l_out):
        def put(p):
            lbv = lb_in[...]
            f = lbv + (1.0 - lbv) * _sigmoid(p)
            k_out[...] = (1.0 - f).astype(BF16)
            l_out[...] = jnp.log2(f)
        return put

    def put_ri(p):
        ri_ref[...] = p.astype(BF16)

    epilogues = (put_q, put_k, put_v, put_silu(rq_ref), put_gate(lbf_ref, kf_ref, lf_ref),
                 put_gate(lbb_ref, kb_ref, lb_ref), put_ri, put_silu(sg_ref))
    ahead = [proj(g) for g in range(INPROJ_LOOKAHEAD)]
    for g, epilogue in enumerate(epilogues):
        if g + INPROJ_LOOKAHEAD < len(epilogues):
            ahead.append(proj(g + INPROJ_LOOKAHEAD))
        epilogue(ahead.pop(0))


def _inproj(x2, nw, w_in, seg, qw, kw, lbf, lbb, kaug, batch):
    n, d = x2.shape
    tm = TOKEN_TILE
    l = n // batch
    tps = l // tm
    aug_tiles = kaug.shape[0] // tm
    row = lambda i: (i, 0)
    out_bf = jax.ShapeDtypeStruct((n, GROUP_W), BF16)
    out_f = jax.ShapeDtypeStruct((n, GROUP_W), F32)
    ospec = pl.BlockSpec((tm, GROUP_W), row)
    kw_cols = kaug.shape[1]
    return pl.pallas_call(
        _inproj_body,
        grid=(n // tm,),
        in_specs=[pl.BlockSpec((tm, d), row), _const_spec((1, d)), _const_spec(w_in.shape),
                  _const_spec(seg.shape)] + [_const_spec((1, GROUP_W))] * 4
                 + [pl.BlockSpec((tm, kw_cols), lambda i: (i % aug_tiles, 0))],
        out_specs=[pl.BlockSpec((None, GROUP_W, tm), lambda i: (i // tps, 0, i % tps)),
                   pl.BlockSpec((tm, kw_cols), row),
                   pl.BlockSpec((None, ATTN_HEADS, ATTN_VROWS, tm),
                                lambda i: (i // tps, 0, 0, i % tps))] + [ospec] * 7,
        out_shape=[jax.ShapeDtypeStruct((batch, GROUP_W, l), BF16),
                   jax.ShapeDtypeStruct((n, kw_cols), BF16),
                   jax.ShapeDtypeStruct((batch, ATTN_HEADS, ATTN_VROWS, l), BF16),
                   out_bf, out_bf, out_bf, out_f, out_f, out_bf, out_bf],
        compiler_params=_cparams(("parallel",)),
    )(x2, nw, w_in, seg, qw, kw, lbf, lbb, kaug)


def _attn_body(c_ref, lam_ref, qt_ref, k_ref, vt_ref, aug_ref, wcol_ref, o_ref, m_ref, acc_ref,
               l_ref, *, bq, bk, nkv, nqb, online):
    c = c_ref[pl.program_id(1)]
    lam = lam_ref[0]
    rows = lax.broadcasted_iota(jnp.int32, (bk, bq), 0)

    if online:
        m_ref[...] = jnp.full(m_ref.shape, -jnp.inf, F32)
        acc_ref[...] = jnp.zeros(acc_ref.shape, F32)

    def blocks_of(qi):
        t0 = (pl.program_id(2) * nqb + qi) * bq
        jd = lax.div(t0, bk)
        tg = t0 + lax.broadcasted_iota(jnp.int32, (1, bq), 1)
        dist = jnp.abs((tg - jd * bk) - rows).astype(F32)
        blocks = [(qi, tg, jd, jnp.zeros((ATTN_QKDIM, bq), BF16), 0.0, dist * (-c))]
        for idx in range(nkv - 1):
            j = idx + (idx >= jd).astype(jnp.int32)
            sgn = jnp.where(j < jd, 1.0, -1.0)
            blocks.append((qi, tg, j, (aug_ref[...] * sgn).astype(BF16), -sgn * c, None))
        return blocks

    def scores(blk, mp):
        qi, _, j, aug, _, bias = blk
        s0 = pl.multiple_of(j * bk, bk)
        q = qt_ref[mp * ATTN_QKDIM:(mp + 1) * ATTN_QKDIM, qi * bq:(qi + 1) * bq]
        st = jnp.dot(k_ref[pl.ds(s0, bk), mp * 128:(mp + 1) * 128],
                     jnp.concatenate([q, aug], axis=0), preferred_element_type=F32)
        return st if bias is None else st + bias

    def fold(blk, mp, st):
        qi, tg, j, _, wcoef, bias = blk
        s0 = pl.multiple_of(j * bk, bk)
        w = wcoef * (tg - s0).astype(F32)
        vt = vt_ref[:, pl.ds(s0, bk)]
        if online:
            m_old = m_ref[qi, mp]
            m_new = jnp.maximum(m_old, jnp.max(st, axis=0, keepdims=True) + w)
            alpha = jnp.exp2(m_old - m_new)
            p = jnp.exp2((st - (m_new - w)).astype(BF16))
            acc_ref[qi, mp] = (acc_ref[qi, mp] * alpha
                               + jnp.dot(vt, p, preferred_element_type=F32))
            m_ref[qi, mp] = m_new
        elif bias is not None:
            shift = jnp.max(st, axis=0, keepdims=True)
            m_ref[qi, mp] = shift
            p = jnp.exp2(st - shift)
            l_ref[qi, mp] = jnp.sum(p.reshape(bk // 8, 8, bq), axis=0)
            acc_ref[qi, mp, :ATTN_VDIM] = jnp.dot(vt_ref[:ATTN_VDIM, pl.ds(s0, bk)], p.astype(BF16),
                                                  preferred_element_type=F32)
        else:
            p = jnp.exp2(st - (m_ref[qi, mp] - w))
            l_ref[qi, mp] += jnp.sum(p.reshape(bk // 8, 8, bq), axis=0)
            acc_ref[qi, mp, :ATTN_VDIM] += jnp.dot(vt_ref[:ATTN_VDIM, pl.ds(s0, bk)], p.astype(BF16),
                                                   preferred_element_type=F32)

    def finish(qi):
        outs = []
        for mp in range(2):
            a = acc_ref[qi, mp]
            if online:
                rowsum = a[ATTN_VDIM:ATTN_VDIM + 1]
            else:
                rowsum = jnp.sum(l_ref[qi, mp], axis=0, keepdims=True)
            outs.append(a[:ATTN_VDIM] * (1.0 / rowsum))
        o = outs[0] - lam * outs[1]
        ms = jnp.mean(o * o, axis=0, keepdims=True)
        y = o * lax.rsqrt(ms + NORM_EPS) * wcol_ref[...] * (1.0 - LAM_INIT)
        o_ref[qi * bq:(qi + 1) * bq, :] = y.T.astype(BF16)

    units = [(blk, mp) for qi in range(nqb) for blk in blocks_of(qi) for mp in range(2)]
    pending = [scores(*u) for u in units[:ATTN_LOOKAHEAD]]
    for n, (blk, mp) in enumerate(units):
        if n + ATTN_LOOKAHEAD < len(units):
            pending.append(scores(*units[n + ATTN_LOOKAHEAD]))
        fold(blk, mp, pending.pop(0))
        if n + 1 == len(units) or units[n + 1][0][0] != blk[0]:
            finish(blk[0])


def _attn(cs, lam, qt, kp, vt, aug, wcol, online):
    b, h, _, l = vt.shape
    bq, bk, nqb = ATTN_BQ, ATTN_BK, ATTN_QBLOCKS
    assert bk % bq == 0 and l % bk == 0 and l % (bq * nqb) == 0
    body = functools.partial(_attn_body, bq=bq, bk=bk, nkv=l // bk, nqb=nqb, online=online)
    smem = pl.BlockSpec(memory_space=pltpu.SMEM)
    return pl.pallas_call(
        body,
        grid=(b, h, l // (bq * nqb)),
        in_specs=[smem, smem,
                  pl.BlockSpec((None, 2 * ATTN_QKDIM, nqb * bq), lambda bi, hi, i: (bi, hi, i)),
                  pl.BlockSpec((None, l, 256), lambda bi, hi, i: (bi, 0, hi)),
                  pl.BlockSpec((None, None, ATTN_VROWS, l), lambda bi, hi, i: (bi, hi, 0, 0)),
                  pl.BlockSpec((None, ATTN_QKDIM, bq), lambda bi, hi, i: (hi, 0, 0)),
                  _const_spec((ATTN_VDIM, bq))],
        out_specs=pl.BlockSpec((None, nqb * bq, ATTN_VDIM), lambda bi, hi, i: (bi, i, hi)),
        out_shape=jax.ShapeDtypeStruct((b, l, h * ATTN_VDIM), BF16),
        scratch_shapes=[pltpu.VMEM((nqb, 2, 1, bq), F32),
                        pltpu.VMEM((nqb, 2, ATTN_VROWS, bq), F32),
                        pltpu.VMEM((nqb, 2, 8, bq), F32)],
        compiler_params=_cparams(("parallel", "parallel", "arbitrary")),
    )(cs, lam, qt, kp, vt, aug, wcol)


def _gla_levels(c, diag):
    lv, half = [], c // 2
    while half >= diag:
        lv.append(half)
        half //= 2
    return lv


def _gla_constants(c, diag):
    r = np.arange(c)[:, None]
    u = np.arange(c)[None, :]
    mats = [(u <= r), (u > r)]
    bmasks = []
    for half in _gla_levels(c, diag):
        blk = r // (2 * half)
        upper = (r % (2 * half)) >= half
        mid = blk * 2 * half + half - 1
        mats.append(np.where(upper, (u > mid) & (u <= r), (u > r) & (u <= mid)))
        t, s = r, u
        bmasks.append(((t // (2 * half)) == (s // (2 * half)))
                      & ((t % (2 * half)) >= half) & ((s % (2 * half)) < half))
    if diag > 8:
        mats.append((u // diag == r // diag) & (u <= r))
    dmask = ((r // diag) == (u // diag)) & (u <= r)
    fwd_m = np.concatenate([m.astype(np.float32) for m in mats], axis=0)
    fwd_b = np.stack([m.astype(np.float32) for m in bmasks])
    fwd_d = dmask.astype(np.float32)
    flip = lambda m: m[::-1, ::-1]
    bwd_m = np.concatenate([flip(m.astype(np.float32)) for m in mats], axis=0)
    bwd_b = np.stack([flip(m.astype(np.float32)) for m in bmasks])
    bwd_d = flip(fwd_d)
    fwd_m, bwd_m = (np.concatenate([m, m], axis=1) for m in (fwd_m, bwd_m))
    return (jnp.asarray(np.stack([fwd_m, bwd_m]), BF16),
            jnp.asarray(np.stack([fwd_b, bwd_b]), F32),
            jnp.asarray(np.stack([fwd_d, bwd_d]), F32))


def _gla_body(qf_ref, kf_ref, vf_ref, lf_ref, qb_ref, kb_ref, vb_ref, lb_ref,
              mst_ref, bm_ref, dm_ref, of_ref, ob_ref, s_ref, *, c, nch, diag):
    nlev = len(_gla_levels(c, diag))
    nt = (((1,), (1,)), ((), ()))

    @pl.when(pl.program_id(1) == 0)
    def _():
        s_ref[...] = jnp.zeros(s_ref.shape, F32)

    lane = lax.broadcasted_iota(jnp.int32, (8, c), 1)
    dirs = ((qf_ref, kf_ref, vf_ref, lf_ref, of_ref), (qb_ref, kb_ref, vb_ref, lb_ref, ob_ref))

    def chunk(ci, carry):
        exps, rows = [], []
        for d, (_, _, _, l_ref, _) in enumerate(dirs):
            cc = ci if d == 0 else nch - 1 - ci
            r0 = pl.multiple_of(cc * c, c)
            lg = l_ref[pl.ds(r0, c), :]
            h1 = lg.astype(BF16)
            h2 = (lg - h1.astype(F32)).astype(BF16)
            exps.append(jnp.dot(mst_ref[d], jnp.concatenate([h1, h2], axis=0),
                                preferred_element_type=F32))
            rows.append(r0)

        units = []
        for hd in range(REC_HEADS):
            cols = slice(hd * REC_DIM, (hd + 1) * REC_DIM)
            for d, (q_ref, k_ref, v_ref, _, _) in enumerate(dirs):
                ex, r0 = exps[d], rows[d]
                g = ex[0:c, cols]
                q = q_ref[pl.ds(r0, c), cols].astype(F32)
                k = k_ref[pl.ds(r0, c), cols].astype(F32)
                st = s_ref[d, hd]
                o = lax.dot_general((q * jnp.exp2(g)).astype(BF16), st.astype(BF16), nt,
                                    preferred_element_type=F32)
                a = jnp.zeros((c, c), F32)
                for li in range(nlev):
                    eh = jnp.exp2(ex[(2 + li) * c:(3 + li) * c, cols])
                    pr = lax.dot_general((q * eh).astype(BF16), (k * eh).astype(BF16), nt,
                                         preferred_element_type=F32)
                    a = a + bm_ref[d, li] * pr
                if diag > 8:
                    ed = ex[(2 + nlev) * c:(3 + nlev) * c, cols]
                    pr = lax.dot_general((q * jnp.exp2(ed)).astype(BF16),
                                         (k * jnp.exp2(-ed)).astype(BF16), nt,
                                         preferred_element_type=F32)
                    a = a + jnp.where(dm_ref[d] > 0.0, pr, 0.0)
                units.append((hd, d, cols, g, q, k, st, o, a))

        diags = []
        for hd, d, cols, g, q, k, st, o, a in units:
            if diag > 8:
                diags.append(None)
                continue
            strips = []
            for blk in range(c // 8):
                rs = slice(8 * blk, 8 * blk + 8)
                gb, qb, kb = g[rs], q[rs], k[rs]
                strip = jnp.zeros((8, c), F32)
                for s in range(8):
                    e = jnp.exp2(gb - gb[s:s + 1])
                    col = jnp.sum(qb * (kb[s:s + 1] * e), axis=-1, keepdims=True)
                    strip = jnp.where(lane == 8 * blk + s, col, strip)
                strips.append(strip)
            diags.append(jnp.where(dm_ref[d] > 0.0, jnp.concatenate(strips, axis=0), 0.0))

        for (hd, d, cols, g, q, k, st, o, a), dg in zip(units, diags):
            v_ref, o_ref = dirs[d][2], dirs[d][4]
            ex, r0 = exps[d], rows[d]
            v = v_ref[pl.ds(r0, c), cols]
            if dg is not None:
                a = a + dg
            o = o + jnp.dot(a.astype(BF16), v, preferred_element_type=F32)
            o_ref[pl.ds(r0, c), cols] = o.astype(BF16)
            kt = (k * jnp.exp2(ex[c:2 * c, cols])).astype(BF16)
            g_all = g[c - 1:c] if d == 0 else g[0:1]
            s_ref[d, hd] = st * jnp.exp2(g_all) + jnp.dot(
                v.astype(F32).T.astype(BF16), kt, preferred_element_type=F32)
        return carry

    lax.fori_loop(0, nch, chunk, 0, unroll=2)


def _gla(rq, kf, kb, ri, lf, lb, diag):
    b, l, w = rq.shape
    c, tc = GLA_CHUNK, GLA_TILE
    n = l // tc
    mst, bm, dm = _gla_constants(c, diag)
    fwd = pl.BlockSpec((None, tc, w), lambda bi, i: (bi, i, 0))
    bwd = pl.BlockSpec((None, tc, w), lambda bi, i: (bi, n - 1 - i, 0))
    out = jax.ShapeDtypeStruct((b, l, w), BF16)
    return pl.pallas_call(
        functools.partial(_gla_body, c=c, nch=tc // c, diag=diag),
        grid=(b, n),
        in_specs=[fwd, fwd, fwd, fwd, bwd, bwd, bwd, bwd,
                  _const_spec(mst.shape), _const_spec(bm.shape), _const_spec(dm.shape)],
        out_specs=[fwd, bwd],
        out_shape=[out, out],
        scratch_shapes=[pltpu.VMEM((2, REC_HEADS, REC_DIM, REC_DIM), F32)],
        compiler_params=_cparams(("parallel", "arbitrary")),
    )(rq, kf, ri, lf, rq, kb, ri, lb, mst, bm, dm)


def _mix_residual(x, ao, o_f, o_b, sg, rw_ref, wo_ref):
    ro = o_f.astype(F32) + o_b.astype(F32)
    parts = []
    for hd in range(REC_HEADS):
        seg = ro[:, hd * REC_DIM:(hd + 1) * REC_DIM]
        ms = jnp.mean(seg * seg, axis=-1, keepdims=True)
        parts.append(seg * lax.rsqrt(ms + NORM_EPS))
    ron = jnp.concatenate(parts, axis=-1) * rw_ref[...] * sg.astype(F32)
    aw = ao.shape[-1]
    mix = (jnp.dot(ao, wo_ref[:aw, :], preferred_element_type=F32)
           + jnp.dot(ron.astype(BF16), wo_ref[aw:, :], preferred_element_type=F32))
    return x + mix


def _mix_ffn_body(*refs, tm, ft, dff, tiles_per_seq):
    rows = [refs[3 * a:3 * a + 3] for a in range(5)]
    (rw_ref, wo_ref, fw_ref, wu_ref, cw_ref, cb_ref, wd_ref, y_ref, acc_ref, u_ref) = refs[15:]
    x, ao, o_f, o_b, sg = [jnp.concatenate([r[...] for r in trio], axis=0) for trio in rows]
    ext = tm + 2 * HALO
    x1 = _mix_residual(x, ao, o_f, o_b, sg, rw_ref, wo_ref)
    ms = jnp.mean(x1 * x1, axis=-1, keepdims=True)
    h2 = x1 * lax.rsqrt(ms + NORM_EPS) * fw_ref[...]
    pos = lax.rem(pl.program_id(0), tiles_per_seq)
    r = lax.broadcasted_iota(jnp.int32, (ext, 1), 0)
    keep = (((r >= HALO) | (pos != 0)) & ((r < HALO + tm) | (pos != tiles_per_seq - 1)))
    hx = jnp.where(keep, h2, 0.0).astype(BF16)
    acc_ref[...] = x1[HALO:HALO + tm]

    def conv(u, col):
        prev = pltpu.roll(u, 1, axis=0)[HALO:HALO + tm]
        nxt = pltpu.roll(u, ext - 1, axis=0)[HALO:HALO + tm]
        cw = cw_ref[:, pl.ds(col, ft)]
        return (cb_ref[:, pl.ds(col, ft)] + prev * cw[0:1] + u[HALO:HALO + tm] * cw[1:2]
                + nxt * cw[2:3])

    def up_dots(f, slot):
        cg = pl.multiple_of(f * ft, ft)
        cu = pl.multiple_of(dff + f * ft, ft)
        u_ref[slot, 0] = jnp.dot(hx, wu_ref[:, pl.ds(cg, ft)], preferred_element_type=F32)
        u_ref[slot, 1] = jnp.dot(hx, wu_ref[:, pl.ds(cu, ft)], preferred_element_type=F32)

    def finish(f, slot):
        cg = pl.multiple_of(f * ft, ft)
        cu = pl.multiple_of(dff + f * ft, ft)
        gate = conv(u_ref[slot, 0], cg)
        up = conv(u_ref[slot, 1], cu)
        act = (gate * _sigmoid(gate) * up).astype(BF16)
        acc_ref[...] += jnp.dot(act, wd_ref[pl.ds(cg, ft), :], preferred_element_type=F32)

    nf = dff // ft
    ahead = FFN_LOOKAHEAD
    slots = ahead + 1

    def rotation(i, carry):
        for u in range(slots):
            up_dots(slots * i + u + ahead, (u + ahead) % slots)
            finish(slots * i + u, u)
        return carry

    for f in range(ahead):
        up_dots(f, f)
    looped = (nf - ahead) // slots
    lax.fori_loop(0, looped, rotation, 0)
    for f in range(slots * looped, nf):
        if f + ahead < nf:
            up_dots(f + ahead, (f + ahead) % slots)
        finish(f, f % slots)
    y_ref[...] = acc_ref[...]


def _mix_ffn(x2, ao, o_f, o_b, sg, rw, w_out, fw, w_up, conv_w, conv_b, w_down, seq_len):
    n, d = x2.shape
    tm, ft = TOKEN_TILE, FFN_FT
    dff = w_down.shape[0]
    hb = tm // HALO
    nhb = n // HALO

    def trio(width):
        return [pl.BlockSpec((HALO, width), lambda i: (jnp.maximum(i * hb - 1, 0), 0)),
                pl.BlockSpec((tm, width), lambda i: (i, 0)),
                pl.BlockSpec((HALO, width), lambda i: (jnp.minimum((i + 1) * hb, nhb - 1), 0))]

    body = functools.partial(_mix_ffn_body, tm=tm, ft=ft, dff=dff, tiles_per_seq=seq_len // tm)
    row_inputs = (x2, ao, o_f, o_b, sg)
    consts = (rw, w_out, fw, w_up, conv_w, conv_b, w_down)
    return pl.pallas_call(
        body,
        grid=(n // tm,),
        in_specs=[spec for a in row_inputs for spec in trio(a.shape[1])]
                 + [_const_spec(a.shape) for a in consts],
        out_specs=pl.BlockSpec((tm, d), lambda i: (i, 0)),
        out_shape=jax.ShapeDtypeStruct((n, d), F32),
        scratch_shapes=[pltpu.VMEM((tm, d), F32),
                        pltpu.VMEM((FFN_LOOKAHEAD + 1, 2, tm + 2 * HALO, ft), F32)],
        compiler_params=_cparams(("parallel",)),
    )(*[a for a in row_inputs for _ in range(3)], *consts)


def _bf16_split3(x):
    x = x.astype(F32)
    a = x.astype(BF16)
    r = x - a.astype(F32)
    b = r.astype(BF16)
    c = (r - b.astype(F32)).astype(BF16)
    return a, b, c


def _alibi_operands():
    start = 2.0 ** (-8.0 / ATTN_HEADS)
    slopes = np.array([start ** (i + 1) for i in range(ATTN_HEADS)], np.float32)
    cs = jnp.asarray(slopes * np.float32(LOG2E), F32)
    groups = GROUP_W // ATTN_QKDIM
    nrows = max(ATTN_BK, TOKEN_TILE)
    pos = np.arange(nrows) % ATTN_BK
    kaug = np.zeros((nrows, groups, 128), np.float32)
    kaug[:, :, ATTN_QKDIM:ATTN_QKDIM + 3] = (pos - pos % 16)[:, None, None]
    kaug[:, :, ATTN_QKDIM + 3:ATTN_QKDIM + 6] = (pos % 16)[:, None, None]
    c1, c2, c3 = _bf16_split3(cs)
    six = jnp.stack([c1, c2, c3, c1, c2, c3], axis=1).astype(F32)
    rows = jnp.zeros((ATTN_HEADS, ATTN_QKDIM), F32).at[:, :6].set(six)
    aug = jnp.broadcast_to(rows[..., None], rows.shape + (ATTN_BQ,))
    return cs, jnp.asarray(kaug.reshape(nrows, groups * 128), F32), aug


def _trunk(x, norm_mix_w, w_in, q_norm_w, k_norm_w, lam, attn_out_norm_w, lb_fwd, lb_bwd,
           rec_out_norm_w, w_out, norm_ffn_w, w_up, conv_w, conv_b, w_down):
    b, l, d = x.shape
    n = b * l
    x2 = x.reshape(n, d)

    seg = np.kron(np.eye(GROUP_W // ATTN_QKDIM, dtype=np.float32),
                  np.full((ATTN_QKDIM, ATTN_QKDIM), 1.0 / ATTN_QKDIM, np.float32))
    reps = GROUP_W // ATTN_QKDIM
    lbf = jnp.cumsum(jax.nn.softmax(lb_fwd.astype(F32), axis=0), axis=0)[0][None]
    lbb = jnp.cumsum(jax.nn.softmax(lb_bwd.astype(F32), axis=0), axis=0)[0][None]
    cs, kaug, aug = _alibi_operands()
    qt, kp, vt, rq, kf, kb, lf, lb, ri, sg = _inproj(
        x2, norm_mix_w[None], w_in.astype(BF16), jnp.asarray(seg, BF16),
        jnp.tile(q_norm_w, reps)[None], jnp.tile(k_norm_w, reps)[None], lbf, lbb, kaug, b)

    wcol = jnp.broadcast_to(attn_out_norm_w.astype(F32)[:, None], (ATTN_VDIM, ATTN_BQ))
    score_bound = (1.02 * ATTN_QKDIM ** 0.5 * LOG2E) * (jnp.max(jnp.abs(q_norm_w))
                                                         * jnp.max(jnp.abs(k_norm_w)))
    attn_args = (cs, lam.reshape(1), qt, kp.reshape(b, l, kp.shape[-1]), vt, aug, wcol)
    ao = lax.cond(2.0 * score_bound <= ATTN_FIXED_SHIFT_RANGE,
                  functools.partial(_attn, online=False),
                  functools.partial(_attn, online=True), *attn_args)

    shape3 = (b, l, GROUP_W)
    decay_log2 = -GLA_FACTORED_DIAG * jnp.log2(jnp.minimum(jnp.min(lbf), jnp.min(lbb)))
    gla_args = [t.reshape(shape3) for t in (rq, kf, kb, ri, lf, lb)]
    o_f, o_b = lax.cond(decay_log2 <= GLA_FACTORED_RANGE,
                        functools.partial(_gla, diag=GLA_FACTORED_DIAG),
                        functools.partial(_gla, diag=8), *gla_args)

    y = _mix_ffn(x2, ao.reshape(n, GROUP_W), o_f.reshape(n, GROUP_W), o_b.reshape(n, GROUP_W),
                 sg, jnp.tile(rec_out_norm_w, REC_HEADS)[None], w_out.astype(BF16),
                 norm_ffn_w[None], w_up.astype(BF16), conv_w, conv_b[None], w_down.astype(BF16), l)
    return y.reshape(b, l, d)


def kernel(x_prompt, x_sample, norm_mix_w, w_in, q_norm_w, k_norm_w, lambda_q1, lambda_k1,
           lambda_q2, lambda_k2, attn_out_norm_w, lb_fwd, lb_bwd, rec_out_norm_w, w_out,
           norm_ffn_w, w_up, conv_w, conv_b, w_down):
    assert norm_mix_w.shape[0] == 1, "single-layer trunk"
    lam = (jnp.exp(jnp.sum(lambda_q1[0].astype(F32) * lambda_k1[0].astype(F32)))
           - jnp.exp(jnp.sum(lambda_q2[0].astype(F32) * lambda_k2[0].astype(F32))) + LAM_INIT)
    params = (norm_mix_w[0], w_in[0], q_norm_w[0], k_norm_w[0], lam, attn_out_norm_w[0],
              lb_fwd, lb_bwd, rec_out_norm_w[0], w_out[0], norm_ffn_w[0], w_up[0], conv_w[0],
              conv_b[0], w_down[0])
    return (_trunk(x_prompt, *params), _trunk(x_sample, *params))
```

```python
import functools
import math

import numpy as np
import jax
import jax.numpy as jnp
from jax import lax
from jax.experimental import pallas as pl
from jax.experimental.pallas import tpu as pltpu

F32 = jnp.float32
BF16 = jnp.bfloat16

NORM_EPS = 1e-6
ATTN_HEADS = 4
ATTN_QKDIM = 64
ATTN_VDIM = 128
REC_HEADS = 4
REC_DIM = 128
GROUP_W = 512
LOG2E = 1.4426950408889634
LAM_INIT = 0.8 - 0.6 * math.exp(-0.3 * 0)

V7X_VMEM_LIMIT = 56 * 1024 * 1024

TOKEN_TILE = 512
INPROJ_LOOKAHEAD = 2
ATTN_BQ = 256
ATTN_BK = 256
ATTN_QBLOCKS = 2
ATTN_LOOKAHEAD = 4
ATTN_FIXED_SHIFT_RANGE = 64.0
ATTN_VROWS = ATTN_VDIM + 16
GLA_CHUNK = 128
GLA_FACTORED_DIAG = 16
GLA_FACTORED_RANGE = 96.0
GLA_TILE = 512
FFN_FT = 256
FFN_LOOKAHEAD = 1
HALO = 16


def _cparams(sem):
    return pltpu.CompilerParams(dimension_semantics=sem, vmem_limit_bytes=V7X_VMEM_LIMIT)


def _sigmoid(x):
    return 0.5 * jnp.tanh(0.5 * x) + 0.5


def _const_spec(shape):
    zeros = (0,) * len(shape)
    return pl.BlockSpec(shape, lambda *_: zeros)


def _inproj_body(x_ref, nw_ref, w_ref, seg_ref, qw_ref, kw_ref, lbf_ref, lbb_ref,
                 kaug_ref, q_ref, k_ref, v_ref, rq_ref, kf_ref, kb_ref, lf_ref, lb_ref, ri_ref,
                 sg_ref):
    x = x_ref[...]
    ms = jnp.mean(x * x, axis=-1, keepdims=True)
    h = (x * lax.rsqrt(ms + NORM_EPS) * nw_ref[...]).astype(BF16)

    def proj(g):
        return jnp.dot(h, w_ref[:, g * GROUP_W:(g + 1) * GROUP_W], preferred_element_type=F32)

    def head_norm(p, w):
        m = jnp.dot((p * p).astype(BF16), seg_ref[...], preferred_element_type=F32)
        return p * lax.rsqrt(m + NORM_EPS) * w


    def put_q(p):
        qn = head_norm(p, qw_ref[...]) * (ATTN_QKDIM ** -0.5 * LOG2E)
        q_ref[...] = qn.T.astype(BF16)

    def put_k(p):
        kn = head_norm(p, kw_ref[...])
        low = lax.broadcasted_iota(jnp.int32, (kn.shape[0], 128), 1) < ATTN_QKDIM
        for t in range(GROUP_W // 128):
            pair = kn[:, t * 128:(t + 1) * 128]
            for half, src in ((0, pair), (1, pltpu.roll(pair, ATTN_QKDIM, axis=1))):
                cols = slice((2 * t + half) * 128, (2 * t + half + 1) * 128)
                k_ref[:, cols] = jnp.where(low, src, kaug_ref[:, cols]).astype(BF16)

    def put_v(p):
        vt = p.T
        tail = (lax.broadcasted_iota(jnp.int32, (ATTN_VROWS - ATTN_VDIM, vt.shape[-1]), 0) == 0)
        for hd in range(ATTN_HEADS):
            v_ref[hd, :ATTN_VDIM, :] = vt[hd * ATTN_VDIM:(hd + 1) * ATTN_VDIM].astype(BF16)
            v_ref[hd, ATTN_VDIM:, :] = tail.astype(BF16)

    def put_silu(ref):
        def put(p):
            ref[...] = (p * _sigmoid(p)).astype(BF16)
        return put

    def put_gate(lb_in, k_out, l_out):
        def put(p):
            lbv = lb_in[...]
            f = lbv + (1.0 - lbv) * _sigmoid(p)
            k_out[...] = (1.0 - f).astype(BF16)
            l_out[...] = jnp.log2(f)
        return put

    def put_ri(p):
        ri_ref[...] = p.astype(BF16)

    epilogues = (put_q, put_k, put_v, put_silu(rq_ref), put_gate(lbf_ref, kf_ref, lf_ref),
                 put_gate(lbb_ref, kb_ref, lb_ref), put_ri, put_silu(sg_ref))
    ahead = [proj(g) for g in range(INPROJ_LOOKAHEAD)]
    for g, epilogue in enumerate(epilogues):
        if g + INPROJ_LOOKAHEAD < len(epilogues):
            ahead.append(proj(g + INPROJ_LOOKAHEAD))
        epilogue(ahead.pop(0))


def _inproj(x2, nw, w_in, seg, qw, kw, lbf, lbb, kaug, batch):
    n, d = x2.shape
    tm = TOKEN_TILE
    l = n // batch
    tps = l // tm
    aug_tiles = kaug.shape[0] // tm
    row = lambda i: (i, 0)
    out_bf = jax.ShapeDtypeStruct((n, GROUP_W), BF16)
    out_f = jax.ShapeDtypeStruct((n, GROUP_W), F32)
    ospec = pl.BlockSpec((tm, GROUP_W), row)
    kw_cols = kaug.shape[1]
    return pl.pallas_call(
        _inproj_body,
        grid=(n // tm,),
        in_specs=[pl.BlockSpec((tm, d), row), _const_spec((1, d)), _const_spec(w_in.shape),
                  _const_spec(seg.shape)] + [_const_spec((1, GROUP_W))] * 4
                 + [pl.BlockSpec((tm, kw_cols), lambda i: (i % aug_tiles, 0))],
        out_specs=[pl.BlockSpec((None, GROUP_W, tm), lambda i: (i // tps, 0, i % tps)),
                   pl.BlockSpec((tm, kw_cols), row),
                   pl.BlockSpec((None, ATTN_HEADS, ATTN_VROWS, tm),
                                lambda i: (i // tps, 0, 0, i % tps))] + [ospec] * 7,
        out_shape=[jax.ShapeDtypeStruct((batch, GROUP_W, l), BF16),
                   jax.ShapeDtypeStruct((n, kw_cols), BF16),
                   jax.ShapeDtypeStruct((batch, ATTN_HEADS, ATTN_VROWS, l), BF16),
                   out_bf, out_bf, out_bf, out_f, out_f, out_bf, out_bf],
        compiler_params=_cparams(("parallel",)),
    )(x2, nw, w_in, seg, qw, kw, lbf, lbb, kaug)


def _attn_body(c_ref, lam_ref, qt_ref, k_ref, vt_ref, aug_ref, wcol_ref, o_ref, m_ref, acc_ref,
               l_ref, *, bq, bk, nkv, nqb, online):
    c = c_ref[pl.program_id(1)]
    lam = lam_ref[0]
    rows = lax.broadcasted_iota(jnp.int32, (bk, bq), 0)

    if online:
        m_ref[...] = jnp.full(m_ref.shape, -jnp.inf, F32)
        acc_ref[...] = jnp.zeros(acc_ref.shape, F32)

    def blocks_of(qi):
        t0 = (pl.program_id(2) * nqb + qi) * bq
        jd = lax.div(t0, bk)
        tg = t0 + lax.broadcasted_iota(jnp.int32, (1, bq), 1)
        dist = jnp.abs((tg - jd * bk) - rows).astype(F32)
        blocks = [(qi, tg, jd, jnp.zeros((ATTN_QKDIM, bq), BF16), 0.0, dist * (-c))]
        for idx in range(nkv - 1):
            j = idx + (idx >= jd).astype(jnp.int32)
            sgn = jnp.where(j < jd, 1.0, -1.0)
            blocks.append((qi, tg, j, (aug_ref[...] * sgn).astype(BF16), -sgn * c, None))
        return blocks

    def scores(blk, mp):
        qi, _, j, aug, _, bias = blk
        s0 = pl.multiple_of(j * bk, bk)
        q = qt_ref[mp * ATTN_QKDIM:(mp + 1) * ATTN_QKDIM, qi * bq:(qi + 1) * bq]
        st = jnp.dot(k_ref[pl.ds(s0, bk), mp * 128:(mp + 1) * 128],
                     jnp.concatenate([q, aug], axis=0), preferred_element_type=F32)
        return st if bias is None else st + bias

    def fold(blk, mp, st):
        qi, tg, j, _, wcoef, bias = blk
        s0 = pl.multiple_of(j * bk, bk)
        w = wcoef * (tg - s0).astype(F32)
        vt = vt_ref[:, pl.ds(s0, bk)]
        if online:
            m_old = m_ref[qi, mp]
            m_new = jnp.maximum(m_old, jnp.max(st, axis=0, keepdims=True) + w)
            alpha = jnp.exp2(m_old - m_new)
            p = jnp.exp2((st - (m_new - w)).astype(BF16))
            acc_ref[qi, mp] = (acc_ref[qi, mp] * alpha
                               + jnp.dot(vt, p, preferred_element_type=F32))
            m_ref[qi, mp] = m_new
        elif bias is not None:
            shift = jnp.max(st, axis=0, keepdims=True)
            m_ref[qi, mp] = shift
            p = jnp.exp2(st - shift)
            l_ref[qi, mp] = jnp.sum(p.reshape(bk // 8, 8, bq), axis=0)
            acc_ref[qi, mp, :ATTN_VDIM] = jnp.dot(vt_ref[:ATTN_VDIM, pl.ds(s0, bk)], p.astype(BF16),
                                                  preferred_element_type=F32)
        else:
            p = jnp.exp2(st - (m_ref[qi, mp] - w))
            l_ref[qi, mp] += jnp.sum(p.reshape(bk // 8, 8, bq), axis=0)
            acc_ref[qi, mp, :ATTN_VDIM] += jnp.dot(vt_ref[:ATTN_VDIM, pl.ds(s0, bk)], p.astype(BF16),
                                                   preferred_element_type=F32)

    def finish(qi):
        outs = []
        for mp in range(2):
            a = acc_ref[qi, mp]
            if online:
                rowsum = a[ATTN_VDIM:ATTN_VDIM + 1]
            else:
                rowsum = jnp.sum(l_ref[qi, mp], axis=0, keepdims=True)
            outs.append(a[:ATTN_VDIM] * (1.0 / rowsum))
        o = outs[0] - lam * outs[1]
        ms = jnp.mean(o * o, axis=0, keepdims=True)
        y = o * lax.rsqrt(ms + NORM_EPS) * wcol_ref[...] * (1.0 - LAM_INIT)
        o_ref[qi * bq:(qi + 1) * bq, :] = y.T.astype(BF16)

    units = [(blk, mp) for qi in range(nqb) for blk in blocks_of(qi) for mp in range(2)]
    pending = [scores(*u) for u in units[:ATTN_LOOKAHEAD]]
    for n, (blk, mp) in enumerate(units):
        if n + ATTN_LOOKAHEAD < len(units):
            pending.append(scores(*units[n + ATTN_LOOKAHEAD]))
        fold(blk, mp, pending.pop(0))
        if n + 1 == len(units) or units[n + 1][0][0] != blk[0]:
            finish(blk[0])


def _attn(cs, lam, qt, kp, vt, aug, wcol, online):
    b, h, _, l = vt.shape
    bq, bk, nqb = ATTN_BQ, ATTN_BK, ATTN_QBLOCKS
    assert bk % bq == 0 and l % bk == 0 and l % (bq * nqb) == 0
    body = functools.partial(_attn_body, bq=bq, bk=bk, nkv=l // bk, nqb=nqb, online=online)
    smem = pl.BlockSpec(memory_space=pltpu.SMEM)
    return pl.pallas_call(
        body,
        grid=(b, h, l // (bq * nqb)),
        in_specs=[smem, smem,
                  pl.BlockSpec((None, 2 * ATTN_QKDIM, nqb * bq), lambda bi, hi, i: (bi, hi, i)),
                  pl.BlockSpec((None, l, 256), lambda bi, hi, i: (bi, 0, hi)),
                  pl.BlockSpec((None, None, ATTN_VROWS, l), lambda bi, hi, i: (bi, hi, 0, 0)),
                  pl.BlockSpec((None, ATTN_QKDIM, bq), lambda bi, hi, i: (hi, 0, 0)),
                  _const_spec((ATTN_VDIM, bq))],
        out_specs=pl.BlockSpec((None, nqb * bq, ATTN_VDIM), lambda bi, hi, i: (bi, i, hi)),
        out_shape=jax.ShapeDtypeStruct((b, l, h * ATTN_VDIM), BF16),
        scratch_shapes=[pltpu.VMEM((nqb, 2, 1, bq), F32),
                        pltpu.VMEM((nqb, 2, ATTN_VROWS, bq), F32),
                        pltpu.VMEM((nqb, 2, 8, bq), F32)],
        compiler_params=_cparams(("parallel", "parallel", "arbitrary")),
    )(cs, lam, qt, kp, vt, aug, wcol)


def _gla_levels(c, diag):
    lv, half = [], c // 2
    while half >= diag:
        lv.append(half)
        half //= 2
    return lv


def _gla_constants(c, diag):
    r = np.arange(c)[:, None]
    u = np.arange(c)[None, :]
    mats = [(u <= r), (u > r)]
    bmasks = []
    for half in _gla_levels(c, diag):
        blk = r // (2 * half)
        upper = (r % (2 * half)) >= half
        mid = blk * 2 * half + half - 1
        mats.append(np.where(upper, (u > mid) & (u <= r), (u > r) & (u <= mid)))
        t, s = r, u
        bmasks.append(((t // (2 * half)) == (s // (2 * half)))
                      & ((t % (2 * half)) >= half) & ((s % (2 * half)) < half))
    if diag > 8:
        mats.append((u // diag == r // diag) & (u <= r))
    dmask = ((r // diag) == (u // diag)) & (u <= r)
    fwd_m = np.concatenate([m.astype(np.float32) for m in mats], axis=0)
    fwd_b = np.stack([m.astype(np.float32) for m in bmasks])
    fwd_d = dmask.astype(np.float32)
    flip = lambda m: m[::-1, ::-1]
    bwd_m = np.concatenate([flip(m.astype(np.float32)) for m in mats], axis=0)
    bwd_b = np.stack([flip(m.astype(np.float32)) for m in bmasks])
    bwd_d = flip(fwd_d)
    fwd_m, bwd_m = (np.concatenate([m, m], axis=1) for m in (fwd_m, bwd_m))
    return (jnp.asarray(np.stack([fwd_m, bwd_m]), BF16),
            jnp.asarray(np.stack([fwd_b, bwd_b]), F32),
            jnp.asarray(np.stack([fwd_d, bwd_d]), F32))


def _gla_body(qf_ref, kf_ref, vf_ref, lf_ref, qb_ref, kb_ref, vb_ref, lb_ref,
              mst_ref, bm_ref, dm_ref, of_ref, ob_ref, s_ref, *, c, nch, diag):
    nlev = len(_gla_levels(c, diag))
    nt = (((1,), (1,)), ((), ()))

    @pl.when(pl.program_id(1) == 0)
    def _():
        s_ref[...] = jnp.zeros(s_ref.shape, F32)

    lane = lax.broadcasted_iota(jnp.int32, (8, c), 1)
    dirs = ((qf_ref, kf_ref, vf_ref, lf_ref, of_ref), (qb_ref, kb_ref, vb_ref, lb_ref, ob_ref))

    def chunk(ci, carry):
        exps, rows = [], []
        for d, (_, _, _, l_ref, _) in enumerate(dirs):
            cc = ci if d == 0 else nch - 1 - ci
            r0 = pl.multiple_of(cc * c, c)
            lg = l_ref[pl.ds(r0, c), :]
            h1 = lg.astype(BF16)
            h2 = (lg - h1.astype(F32)).astype(BF16)
            exps.append(jnp.dot(mst_ref[d], jnp.concatenate([h1, h2], axis=0),
                                preferred_element_type=F32))
            rows.append(r0)

        units = []
        for hd in range(REC_HEADS):
            cols = slice(hd * REC_DIM, (hd + 1) * REC_DIM)
            for d, (q_ref, k_ref, v_ref, _, _) in enumerate(dirs):
                ex, r0 = exps[d], rows[d]
                g = ex[0:c, cols]
                q = q_ref[pl.ds(r0, c), cols].astype(F32)
                k = k_ref[pl.ds(r0, c), cols].astype(F32)
                st = s_ref[d, hd]
                o = lax.dot_general((q * jnp.exp2(g)).astype(BF16), st.astype(BF16), nt,
                                    preferred_element_type=F32)
                a = jnp.zeros((c, c), F32)
                for li in range(nlev):
                    eh = jnp.exp2(ex[(2 + li) * c:(3 + li) * c, cols])
                    pr = lax.dot_general((q * eh).astype(BF16), (k * eh).astype(BF16), nt,
                                         preferred_element_type=F32)
                    a = a + bm_ref[d, li] * pr
                if diag > 8:
                    ed = ex[(2 + nlev) * c:(3 + nlev) * c, cols]
                    pr = lax.dot_general((q * jnp.exp2(ed)).astype(BF16),
                                         (k * jnp.exp2(-ed)).astype(BF16), nt,
                                         preferred_element_type=F32)
                    a = a + jnp.where(dm_ref[d] > 0.0, pr, 0.0)
                units.append((hd, d, cols, g, q, k, st, o, a))

        diags = []
        for hd, d, cols, g, q, k, st, o, a in units:
            if diag > 8:
                diags.append(None)
                continue
            strips = []
            for blk in range(c // 8):
                rs = slice(8 * blk, 8 * blk + 8)
                gb, qb, kb = g[rs], q[rs], k[rs]
                strip = jnp.zeros((8, c), F32)
                for s in range(8):
                    e = jnp.exp2(gb - gb[s:s + 1])
                    col = jnp.sum(qb * (kb[s:s + 1] * e), axis=-1, keepdims=True)
                    strip = jnp.where(lane == 8 * blk + s, col, strip)
                strips.append(strip)
            diags.append(jnp.where(dm_ref[d] > 0.0, jnp.concatenate(strips, axis=0), 0.0))

        for (hd, d, cols, g, q, k, st, o, a), dg in zip(units, diags):
            v_ref, o_ref = dirs[d][2], dirs[d][4]
            ex, r0 = exps[d], rows[d]
            v = v_ref[pl.ds(r0, c), cols]
            if dg is not None:
                a = a + dg
            o = o + jnp.dot(a.astype(BF16), v, preferred_element_type=F32)
            o_ref[pl.ds(r0, c), cols] = o.astype(BF16)
            kt = (k * jnp.exp2(ex[c:2 * c, cols])).astype(BF16)
            g_all = g[c - 1:c] if d == 0 else g[0:1]
            s_ref[d, hd] = st * jnp.exp2(g_all) + jnp.dot(
                v.astype(F32).T.astype(BF16), kt, preferred_element_type=F32)
        return carry

    lax.fori_loop(0, nch, chunk, 0, unroll=2)


def _gla(rq, kf, kb, ri, lf, lb, diag):
    b, l, w = rq.shape
    c, tc = GLA_CHUNK, GLA_TILE
    n = l // tc
    mst, bm, dm = _gla_constants(c, diag)
    fwd = pl.BlockSpec((None, tc, w), lambda bi, i: (bi, i, 0))
    bwd = pl.BlockSpec((None, tc, w), lambda bi, i: (bi, n - 1 - i, 0))
    out = jax.ShapeDtypeStruct((b, l, w), BF16)
    return pl.pallas_call(
        functools.partial(_gla_body, c=c, nch=tc // c, diag=diag),
        grid=(b, n),
        in_specs=[fwd, fwd, fwd, fwd, bwd, bwd, bwd, bwd,
                  _const_spec(mst.shape), _const_spec(bm.shape), _const_spec(dm.shape)],
        out_specs=[fwd, bwd],
        out_shape=[out, out],
        scratch_shapes=[pltpu.VMEM((2, REC_HEADS, REC_DIM, REC_DIM), F32)],
        compiler_params=_cparams(("parallel", "arbitrary")),
    )(rq, kf, ri, lf, rq, kb, ri, lb, mst, bm, dm)


def _mix_residual(x, ao, o_f, o_b, sg, rw_ref, wo_ref):
    ro = o_f.astype(F32) + o_b.astype(F32)
    parts = []
    for hd in range(REC_HEADS):
        seg = ro[:, hd * REC_DIM:(hd + 1) * REC_DIM]
        ms = jnp.mean(seg * seg, axis=-1, keepdims=True)
        parts.append(seg * lax.rsqrt(ms + NORM_EPS))
    ron = jnp.concatenate(parts, axis=-1) * rw_ref[...] * sg.astype(F32)
    aw = ao.shape[-1]
    mix = (jnp.dot(ao, wo_ref[:aw, :], preferred_element_type=F32)
           + jnp.dot(ron.astype(BF16), wo_ref[aw:, :], preferred_element_type=F32))
    return x + mix


def _mix_ffn_body(*refs, tm, ft, dff, tiles_per_seq):
    rows = [refs[3 * a:3 * a + 3] for a in range(5)]
    (rw_ref, wo_ref, fw_ref, wu_ref, cw_ref, cb_ref, wd_ref, y_ref, acc_ref, u_ref,
     act_ref) = refs[15:]
    x, ao, o_f, o_b, sg = [jnp.concatenate([r[...] for r in trio], axis=0) for trio in rows]
    ext = tm + 2 * HALO
    x1 = _mix_residual(x, ao, o_f, o_b, sg, rw_ref, wo_ref)
    ms = jnp.mean(x1 * x1, axis=-1, keepdims=True)
    h2 = x1 * lax.rsqrt(ms + NORM_EPS) * fw_ref[...]
    pos = lax.rem(pl.program_id(0), tiles_per_seq)
    r = lax.broadcasted_iota(jnp.int32, (ext, 1), 0)
    keep = (((r >= HALO) | (pos != 0)) & ((r < HALO + tm) | (pos != tiles_per_seq - 1)))
    hx = jnp.where(keep, h2, 0.0).astype(BF16)
    acc_ref[...] = x1[HALO:HALO + tm]

    def conv(u, col):
        prev = pltpu.roll(u, 1, axis=0)[HALO:HALO + tm]
        nxt = pltpu.roll(u, ext - 1, axis=0)[HALO:HALO + tm]
        cw = cw_ref[:, pl.ds(col, ft)]
        return (cb_ref[:, pl.ds(col, ft)] + prev * cw[0:1] + u[HALO:HALO + tm] * cw[1:2]
                + nxt * cw[2:3])

    def up_dots(f, slot):
        cg = pl.multiple_of(f * ft, ft)
        cu = pl.multiple_of(dff + f * ft, ft)
        u_ref[slot, 0] = jnp.dot(hx, wu_ref[:, pl.ds(cg, ft)], preferred_element_type=F32)
        u_ref[slot, 1] = jnp.dot(hx, wu_ref[:, pl.ds(cu, ft)], preferred_element_type=F32)

    def finish(f, slot):
        cg = pl.multiple_of(f * ft, ft)
        cu = pl.multiple_of(dff + f * ft, ft)
        gate = conv(u_ref[slot, 0], cg)
        up = conv(u_ref[slot, 1], cu)
        act_ref[:, pl.ds(cg, ft)] = (gate * _sigmoid(gate) * up).astype(BF16)

    nf = dff // ft
    ahead = FFN_LOOKAHEAD
    slots = ahead + 1

    def rotation(i, carry):
        for u in range(slots):
            up_dots(slots * i + u + ahead, (u + ahead) % slots)
            finish(slots * i + u, u)
        return carry

    for f in range(ahead):
        up_dots(f, f)
    looped = (nf - ahead) // slots
    lax.fori_loop(0, looped, rotation, 0)
    for f in range(slots * looped, nf):
        if f + ahead < nf:
            up_dots(f + ahead, (f + ahead) % slots)
        finish(f, f % slots)
    y_ref[...] = acc_ref[...] + jnp.dot(act_ref[...], wd_ref[...], preferred_element_type=F32)


def _mix_ffn(x2, ao, o_f, o_b, sg, rw, w_out, fw, w_up, conv_w, conv_b, w_down, seq_len):
    n, d = x2.shape
    tm, ft = TOKEN_TILE, FFN_FT
    dff = w_down.shape[0]
    hb = tm // HALO
    nhb = n // HALO

    def trio(width):
        return [pl.BlockSpec((HALO, width), lambda i: (jnp.maximum(i * hb - 1, 0), 0)),
                pl.BlockSpec((tm, width), lambda i: (i, 0)),
                pl.BlockSpec((HALO, width), lambda i: (jnp.minimum((i + 1) * hb, nhb - 1), 0))]

    body = functools.partial(_mix_ffn_body, tm=tm, ft=ft, dff=dff, tiles_per_seq=seq_len // tm)
    row_inputs = (x2, ao, o_f, o_b, sg)
    consts = (rw, w_out, fw, w_up, conv_w, conv_b, w_down)
    return pl.pallas_call(
        body,
        grid=(n // tm,),
        in_specs=[spec for a in row_inputs for spec in trio(a.shape[1])]
                 + [_const_spec(a.shape) for a in consts],
        out_specs=pl.BlockSpec((tm, d), lambda i: (i, 0)),
        out_shape=jax.ShapeDtypeStruct((n, d), F32),
        scratch_shapes=[pltpu.VMEM((tm, d), F32),
                        pltpu.VMEM((FFN_LOOKAHEAD + 1, 2, tm + 2 * HALO, ft), F32),
                        pltpu.VMEM((tm, dff), BF16)],
        compiler_params=_cparams(("parallel",)),
    )(*[a for a in row_inputs for _ in range(3)], *consts)


def _bf16_split3(x):
    x = x.astype(F32)
    a = x.astype(BF16)
    r = x - a.astype(F32)
    b = r.astype(BF16)
    c = (r - b.astype(F32)).astype(BF16)
    return a, b, c


def _alibi_operands():
    start = 2.0 ** (-8.0 / ATTN_HEADS)
    slopes = np.array([start ** (i + 1) for i in range(ATTN_HEADS)], np.float32)
    cs = jnp.asarray(slopes * np.float32(LOG2E), F32)
    groups = GROUP_W // ATTN_QKDIM
    nrows = max(ATTN_BK, TOKEN_TILE)
    pos = np.arange(nrows) % ATTN_BK
    kaug = np.zeros((nrows, groups, 128), np.float32)
    kaug[:, :, ATTN_QKDIM:ATTN_QKDIM + 3] = (pos - pos % 16)[:, None, None]
    kaug[:, :, ATTN_QKDIM + 3:ATTN_QKDIM + 6] = (pos % 16)[:, None, None]
    c1, c2, c3 = _bf16_split3(cs)
    six = jnp.stack([c1, c2, c3, c1, c2, c3], axis=1).astype(F32)
    rows = jnp.zeros((ATTN_HEADS, ATTN_QKDIM), F32).at[:, :6].set(six)
    aug = jnp.broadcast_to(rows[..., None], rows.shape + (ATTN_BQ,))
    return cs, jnp.asarray(kaug.reshape(nrows, groups * 128), F32), aug


def _trunk(x, norm_mix_w, w_in, q_norm_w, k_norm_w, lam, attn_out_norm_w, lb_fwd, lb_bwd,
           rec_out_norm_w, w_out, norm_ffn_w, w_up, conv_w, conv_b, w_down):
    b, l, d = x.shape
    n = b * l
    x2 = x.reshape(n, d)

    seg = np.kron(np.eye(GROUP_W // ATTN_QKDIM, dtype=np.float32),
                  np.full((ATTN_QKDIM, ATTN_QKDIM), 1.0 / ATTN_QKDIM, np.float32))
    reps = GROUP_W // ATTN_QKDIM
    lbf = jnp.cumsum(jax.nn.softmax(lb_fwd.astype(F32), axis=0), axis=0)[0][None]
    lbb = jnp.cumsum(jax.nn.softmax(lb_bwd.astype(F32), axis=0), axis=0)[0][None]
    cs, kaug, aug = _alibi_operands()
    qt, kp, vt, rq, kf, kb, lf, lb, ri, sg = _inproj(
        x2, norm_mix_w[None], w_in.astype(BF16), jnp.asarray(seg, BF16),
        jnp.tile(q_norm_w, reps)[None], jnp.tile(k_norm_w, reps)[None], lbf, lbb, kaug, b)

    wcol = jnp.broadcast_to(attn_out_norm_w.astype(F32)[:, None], (ATTN_VDIM, ATTN_BQ))
    score_bound = (1.02 * ATTN_QKDIM ** 0.5 * LOG2E) * (jnp.max(jnp.abs(q_norm_w))
                                                         * jnp.max(jnp.abs(k_norm_w)))
    attn_args = (cs, lam.reshape(1), qt, kp.reshape(b, l, kp.shape[-1]), vt, aug, wcol)
    ao = lax.cond(2.0 * score_bound <= ATTN_FIXED_SHIFT_RANGE,
                  functools.partial(_attn, online=False),
                  functools.partial(_attn, online=True), *attn_args)

    shape3 = (b, l, GROUP_W)
    decay_log2 = -GLA_FACTORED_DIAG * jnp.log2(jnp.minimum(jnp.min(lbf), jnp.min(lbb)))
    gla_args = [t.reshape(shape3) for t in (rq, kf, kb, ri, lf, lb)]
    o_f, o_b = lax.cond(decay_log2 <= GLA_FACTORED_RANGE,
                        functools.partial(_gla, diag=GLA_FACTORED_DIAG),
                        functools.partial(_gla, diag=8), *gla_args)

    y = _mix_ffn(x2, ao.reshape(n, GROUP_W), o_f.reshape(n, GROUP_W), o_b.reshape(n, GROUP_W),
                 sg, jnp.tile(rec_out_norm_w, REC_HEADS)[None], w_out.astype(BF16),
                 norm_ffn_w[None], w_up.astype(BF16), conv_w, conv_b[None], w_down.astype(BF16), l)
    return y.reshape(b, l, d)


def kernel(x_prompt, x_sample, norm_mix_w, w_in, q_norm_w, k_norm_w, lambda_q1, lambda_k1,
           lambda_q2, lambda_k2, attn_out_norm_w, lb_fwd, lb_bwd, rec_out_norm_w, w_out,
           norm_ffn_w, w_up, conv_w, conv_b, w_down):
    assert norm_mix_w.shape[0] == 1, "single-layer trunk"
    lam = (jnp.exp(jnp.sum(lambda_q1[0].astype(F32) * lambda_k1[0].astype(F32)))
           - jnp.exp(jnp.sum(lambda_q2[0].astype(F32) * lambda_k2[0].astype(F32))) + LAM_INIT)
    params = (norm_mix_w[0], w_in[0], q_norm_w[0], k_norm_w[0], lam, attn_out_norm_w[0],
              lb_fwd, lb_bwd, rec_out_norm_w[0], w_out[0], norm_ffn_w[0], w_up[0], conv_w[0],
              conv_b[0], w_down[0])
    return (_trunk(x_prompt, *params), _trunk(x_sample, *params))
```

```python
import functools
import math

import numpy as np
import jax
import jax.numpy as jnp
from jax import lax
from jax.experimental import pallas as pl
from jax.experimental.pallas import tpu as pltpu

F32 = jnp.float32
BF16 = jnp.bfloat16

NORM_EPS = 1e-6
ATTN_HEADS = 4
ATTN_QKDIM = 64
ATTN_VDIM = 128
REC_HEADS = 4
REC_DIM = 128
GROUP_W = 512
LOG2E = 1.4426950408889634
LAM_INIT = 0.8 - 0.6 * math.exp(-0.3 * 0)

V7X_VMEM_LIMIT = 56 * 1024 * 1024

TOKEN_TILE = 512
INPROJ_LOOKAHEAD = 2
ATTN_BQ = 256
ATTN_BK = 256
ATTN_QBLOCKS = 2
ATTN_LOOKAHEAD = 4
ATTN_FIXED_SHIFT_RANGE = 64.0
ATTN_VROWS = ATTN_VDIM + 16
GLA_CHUNK = 128
GLA_FACTORED_DIAG = 16
GLA_FACTORED_RANGE = 96.0
GLA_TILE = 512
FFN_FT = 256
FFN_LOOKAHEAD = 2
HALO = 16


def _cparams(sem):
    return pltpu.CompilerParams(dimension_semantics=sem, vmem_limit_bytes=V7X_VMEM_LIMIT)


def _sigmoid(x):
    return 0.5 * jnp.tanh(0.5 * x) + 0.5


def _const_spec(shape):
    zeros = (0,) * len(shape)
    return pl.BlockSpec(shape, lambda *_: zeros)


def _inproj_body(x_ref, nw_ref, w_ref, seg_ref, qw_ref, kw_ref, lbf_ref, lbb_ref,
                 kaug_ref, q_ref, k_ref, v_ref, rq_ref, kf_ref, kb_ref, lf_ref, lb_ref, ri_ref,
                 sg_ref):
    x = x_ref[...]
    ms = jnp.mean(x * x, axis=-1, keepdims=True)
    h = (x * lax.rsqrt(ms + NORM_EPS) * nw_ref[...]).astype(BF16)

    def proj(g):
        return jnp.dot(h, w_ref[:, g * GROUP_W:(g + 1) * GROUP_W], preferred_element_type=F32)

    def head_norm(p, w):
        m = jnp.dot((p * p).astype(BF16), seg_ref[...], preferred_element_type=F32)
        return p * lax.rsqrt(m + NORM_EPS) * w


    def put_q(p):
        qn = head_norm(p, qw_ref[...]) * (ATTN_QKDIM ** -0.5 * LOG2E)
        q_ref[...] = qn.T.astype(BF16)

    def put_k(p):
        kn = head_norm(p, kw_ref[...])
        low = lax.broadcasted_iota(jnp.int32, (kn.shape[0], 128), 1) < ATTN_QKDIM
        for t in range(GROUP_W // 128):
            pair = kn[:, t * 128:(t + 1) * 128]
            for half, src in ((0, pair), (1, pltpu.roll(pair, ATTN_QKDIM, axis=1))):
                cols = slice((2 * t + half) * 128, (2 * t + half + 1) * 128)
                k_ref[:, cols] = jnp.where(low, src, kaug_ref[:, cols]).astype(BF16)

    def put_v(p):
        vt = p.T
        tail = (lax.broadcasted_iota(jnp.int32, (ATTN_VROWS - ATTN_VDIM, vt.shape[-1]), 0) == 0)
        for hd in range(ATTN_HEADS):
            v_ref[hd, :ATTN_VDIM, :] = vt[hd * ATTN_VDIM:(hd + 1) * ATTN_VDIM].astype(BF16)
            v_ref[hd, ATTN_VDIM:, :] = tail.astype(BF16)

    def put_silu(ref):
        def put(p):
            ref[...] = (p * _sigmoid(p)).astype(BF16)
        return put

    def put_gate(lb_in, k_out, l_out):
        def put(p):
            lbv = lb_in[...]
            f = lbv + (1.0 - lbv) * _sigmoid(p)
            k_out[...] = (1.0 - f).astype(BF16)
            l_out[...] = jnp.log2(f)
        return put

    def put_ri(p):
        ri_ref[...] = p.astype(BF16)

    epilogues = (put_q, put_k, put_v, put_silu(rq_ref), put_gate(lbf_ref, kf_ref, lf_ref),
                 put_gate(lbb_ref, kb_ref, lb_ref), put_ri, put_silu(sg_ref))
    ahead = [proj(g) for g in range(INPROJ_LOOKAHEAD)]
    for g, epilogue in enumerate(epilogues):
        if g + INPROJ_LOOKAHEAD < len(epilogues):
            ahead.append(proj(g + INPROJ_LOOKAHEAD))
        epilogue(ahead.pop(0))


def _inproj(x2, nw, w_in, seg, qw, kw, lbf, lbb, kaug, batch):
    n, d = x2.shape
    tm = TOKEN_TILE
    l = n // batch
    tps = l // tm
    aug_tiles = kaug.shape[0] // tm
    row = lambda i: (i, 0)
    out_bf = jax.ShapeDtypeStruct((n, GROUP_W), BF16)
    out_f = jax.ShapeDtypeStruct((n, GROUP_W), F32)
    ospec = pl.BlockSpec((tm, GROUP_W), row)
    kw_cols = kaug.shape[1]
    return pl.pallas_call(
        _inproj_body,
        grid=(n // tm,),
        in_specs=[pl.BlockSpec((tm, d), row), _const_spec((1, d)), _const_spec(w_in.shape),
                  _const_spec(seg.shape)] + [_const_spec((1, GROUP_W))] * 4
                 + [pl.BlockSpec((tm, kw_cols), lambda i: (i % aug_tiles, 0))],
        out_specs=[pl.BlockSpec((None, GROUP_W, tm), lambda i: (i // tps, 0, i % tps)),
                   pl.BlockSpec((tm, kw_cols), row),
                   pl.BlockSpec((None, ATTN_HEADS, ATTN_VROWS, tm),
                                lambda i: (i // tps, 0, 0, i % tps))] + [ospec] * 7,
        out_shape=[jax.ShapeDtypeStruct((batch, GROUP_W, l), BF16),
                   jax.ShapeDtypeStruct((n, kw_cols), BF16),
                   jax.ShapeDtypeStruct((batch, ATTN_HEADS, ATTN_VROWS, l), BF16),
                   out_bf, out_bf, out_bf, out_f, out_f, out_bf, out_bf],
        compiler_params=_cparams(("parallel",)),
    )(x2, nw, w_in, seg, qw, kw, lbf, lbb, kaug)


def _attn_body(c_ref, lam_ref, qt_ref, k_ref, vt_ref, aug_ref, wcol_ref, o_ref, m_ref, acc_ref,
               l_ref, *, bq, bk, nkv, nqb, online):
    c = c_ref[pl.program_id(1)]
    lam = lam_ref[0]
    rows = lax.broadcasted_iota(jnp.int32, (bk, bq), 0)

    if online:
        m_ref[...] = jnp.full(m_ref.shape, -jnp.inf, F32)
        acc_ref[...] = jnp.zeros(acc_ref.shape, F32)

    def blocks_of(qi):
        t0 = (pl.program_id(2) * nqb + qi) * bq
        jd = lax.div(t0, bk)
        tg = t0 + lax.broadcasted_iota(jnp.int32, (1, bq), 1)
        dist = jnp.abs((tg - jd * bk) - rows).astype(F32)
        blocks = [(qi, tg, jd, jnp.zeros((ATTN_QKDIM, bq), BF16), 0.0, dist * (-c))]
        for idx in range(nkv - 1):
            j = idx + (idx >= jd).astype(jnp.int32)
            sgn = jnp.where(j < jd, 1.0, -1.0)
            blocks.append((qi, tg, j, (aug_ref[...] * sgn).astype(BF16), -sgn * c, None))
        return blocks

    def scores(blk, mp):
        qi, _, j, aug, _, bias = blk
        s0 = pl.multiple_of(j * bk, bk)
        q = qt_ref[mp * ATTN_QKDIM:(mp + 1) * ATTN_QKDIM, qi * bq:(qi + 1) * bq]
        st = jnp.dot(k_ref[pl.ds(s0, bk), mp * 128:(mp + 1) * 128],
                     jnp.concatenate([q, aug], axis=0), preferred_element_type=F32)
        return st if bias is None else st + bias

    def fold(blk, mp, st):
        qi, tg, j, _, wcoef, bias = blk
        s0 = pl.multiple_of(j * bk, bk)
        w = wcoef * (tg - s0).astype(F32)
        vt = vt_ref[:, pl.ds(s0, bk)]
        if online:
            m_old = m_ref[qi, mp]
            m_new = jnp.maximum(m_old, jnp.max(st, axis=0, keepdims=True) + w)
            alpha = jnp.exp2(m_old - m_new)
            p = jnp.exp2((st - (m_new - w)).astype(BF16))
            acc_ref[qi, mp] = (acc_ref[qi, mp] * alpha
                               + jnp.dot(vt, p, preferred_element_type=F32))
            m_ref[qi, mp] = m_new
        elif bias is not None:
            shift = jnp.max(st, axis=0, keepdims=True)
            m_ref[qi, mp] = shift
            p = jnp.exp2(st - shift)
            l_ref[qi, mp] = jnp.sum(p.reshape(bk // 8, 8, bq), axis=0)
            acc_ref[qi, mp, :ATTN_VDIM] = jnp.dot(vt_ref[:ATTN_VDIM, pl.ds(s0, bk)], p.astype(BF16),
                                                  preferred_element_type=F32)
        else:
            p = jnp.exp2(st - (m_ref[qi, mp] - w))
            l_ref[qi, mp] += jnp.sum(p.reshape(bk // 8, 8, bq), axis=0)
            acc_ref[qi, mp, :ATTN_VDIM] += jnp.dot(vt_ref[:ATTN_VDIM, pl.ds(s0, bk)], p.astype(BF16),
                                                   preferred_element_type=F32)

    def finish(qi):
        outs = []
        for mp in range(2):
            a = acc_ref[qi, mp]
            if online:
                rowsum = a[ATTN_VDIM:ATTN_VDIM + 1]
            else:
                rowsum = jnp.sum(l_ref[qi, mp], axis=0, keepdims=True)
            outs.append(a[:ATTN_VDIM] * (1.0 / rowsum))
        o = outs[0] - lam * outs[1]
        ms = jnp.mean(o * o, axis=0, keepdims=True)
        y = o * lax.rsqrt(ms + NORM_EPS) * wcol_ref[...] * (1.0 - LAM_INIT)
        o_ref[qi * bq:(qi + 1) * bq, :] = y.T.astype(BF16)

    units = [(blk, mp) for qi in range(nqb) for blk in blocks_of(qi) for mp in range(2)]
    pending = [scores(*u) for u in units[:ATTN_LOOKAHEAD]]
    for n, (blk, mp) in enumerate(units):
        if n + ATTN_LOOKAHEAD < len(units):
            pending.append(scores(*units[n + ATTN_LOOKAHEAD]))
        fold(blk, mp, pending.pop(0))
        if n + 1 == len(units) or units[n + 1][0][0] != blk[0]:
            finish(blk[0])


def _attn(cs, lam, qt, kp, vt, aug, wcol, online):
    b, h, _, l = vt.shape
    bq, bk, nqb = ATTN_BQ, ATTN_BK, ATTN_QBLOCKS
    assert bk % bq == 0 and l % bk == 0 and l % (bq * nqb) == 0
    body = functools.partial(_attn_body, bq=bq, bk=bk, nkv=l // bk, nqb=nqb, online=online)
    smem = pl.BlockSpec(memory_space=pltpu.SMEM)
    return pl.pallas_call(
        body,
        grid=(b, h, l // (bq * nqb)),
        in_specs=[smem, smem,
                  pl.BlockSpec((None, 2 * ATTN_QKDIM, nqb * bq), lambda bi, hi, i: (bi, hi, i)),
                  pl.BlockSpec((None, l, 256), lambda bi, hi, i: (bi, 0, hi)),
                  pl.BlockSpec((None, None, ATTN_VROWS, l), lambda bi, hi, i: (bi, hi, 0, 0)),
                  pl.BlockSpec((None, ATTN_QKDIM, bq), lambda bi, hi, i: (hi, 0, 0)),
                  _const_spec((ATTN_VDIM, bq))],
        out_specs=pl.BlockSpec((None, nqb * bq, ATTN_VDIM), lambda bi, hi, i: (bi, i, hi)),
        out_shape=jax.ShapeDtypeStruct((b, l, h * ATTN_VDIM), BF16),
        scratch_shapes=[pltpu.VMEM((nqb, 2, 1, bq), F32),
                        pltpu.VMEM((nqb, 2, ATTN_VROWS, bq), F32),
                        pltpu.VMEM((nqb, 2, 8, bq), F32)],
        compiler_params=_cparams(("parallel", "parallel", "arbitrary")),
    )(cs, lam, qt, kp, vt, aug, wcol)


def _gla_levels(c, diag):
    lv, half = [], c // 2
    while half >= diag:
        lv.append(half)
        half //= 2
    return lv


def _gla_constants(c, diag):
    r = np.arange(c)[:, None]
    u = np.arange(c)[None, :]
    mats = [(u <= r), (u > r)]
    bmasks = []
    for half in _gla_levels(c, diag):
        blk = r // (2 * half)
        upper = (r % (2 * half)) >= half
        mid = blk * 2 * half + half - 1
        mats.append(np.where(upper, (u > mid) & (u <= r), (u > r) & (u <= mid)))
        t, s = r, u
        bmasks.append(((t // (2 * half)) == (s // (2 * half)))
                      & ((t % (2 * half)) >= half) & ((s % (2 * half)) < half))
    if diag > 8:
        mats.append((u // diag == r // diag) & (u <= r))
    dmask = ((r // diag) == (u // diag)) & (u <= r)
    fwd_m = np.concatenate([m.astype(np.float32) for m in mats], axis=0)
    fwd_b = np.stack([m.astype(np.float32) for m in bmasks])
    fwd_d = dmask.astype(np.float32)
    flip = lambda m: m[::-1, ::-1]
    bwd_m = np.concatenate([flip(m.astype(np.float32)) for m in mats], axis=0)
    bwd_b = np.stack([flip(m.astype(np.float32)) for m in bmasks])
    bwd_d = flip(fwd_d)
    fwd_m, bwd_m = (np.concatenate([m, m], axis=1) for m in (fwd_m, bwd_m))
    return (jnp.asarray(np.stack([fwd_m, bwd_m]), BF16),
            jnp.asarray(np.stack([fwd_b, bwd_b]), F32),
            jnp.asarray(np.stack([fwd_d, bwd_d]), F32))


def _gla_body(qf_ref, kf_ref, vf_ref, lf_ref, qb_ref, kb_ref, vb_ref, lb_ref,
              mst_ref, bm_ref, dm_ref, of_ref, ob_ref, s_ref, *, c, nch, diag):
    nlev = len(_gla_levels(c, diag))
    nt = (((1,), (1,)), ((), ()))

    @pl.when(pl.program_id(1) == 0)
    def _():
        s_ref[...] = jnp.zeros(s_ref.shape, F32)

    lane = lax.broadcasted_iota(jnp.int32, (8, c), 1)
    dirs = ((qf_ref, kf_ref, vf_ref, lf_ref, of_ref), (qb_ref, kb_ref, vb_ref, lb_ref, ob_ref))

    def chunk(ci, carry):
        exps, rows = [], []
        for d, (_, _, _, l_ref, _) in enumerate(dirs):
            cc = ci if d == 0 else nch - 1 - ci
            r0 = pl.multiple_of(cc * c, c)
            lg = l_ref[pl.ds(r0, c), :]
            h1 = lg.astype(BF16)
            h2 = (lg - h1.astype(F32)).astype(BF16)
            exps.append(jnp.dot(mst_ref[d], jnp.concatenate([h1, h2], axis=0),
                                preferred_element_type=F32))
            rows.append(r0)

        units = []
        for hd in range(REC_HEADS):
            cols = slice(hd * REC_DIM, (hd + 1) * REC_DIM)
            for d, (q_ref, k_ref, v_ref, _, _) in enumerate(dirs):
                ex, r0 = exps[d], rows[d]
                g = ex[0:c, cols]
                q = q_ref[pl.ds(r0, c), cols].astype(F32)
                k = k_ref[pl.ds(r0, c), cols].astype(F32)
                st = s_ref[d, hd]
                o = lax.dot_general((q * jnp.exp2(g)).astype(BF16), st.astype(BF16), nt,
                                    preferred_element_type=F32)
                a = jnp.zeros((c, c), F32)
                for li in range(nlev):
                    eh = jnp.exp2(ex[(2 + li) * c:(3 + li) * c, cols])
                    pr = lax.dot_general((q * eh).astype(BF16), (k * eh).astype(BF16), nt,
                                         preferred_element_type=F32)
                    a = a + bm_ref[d, li] * pr
                if diag > 8:
                    ed = ex[(2 + nlev) * c:(3 + nlev) * c, cols]
                    pr = lax.dot_general((q * jnp.exp2(ed)).astype(BF16),
                                         (k * jnp.exp2(-ed)).astype(BF16), nt,
                                         preferred_element_type=F32)
                    a = a + jnp.where(dm_ref[d] > 0.0, pr, 0.0)
                units.append((hd, d, cols, g, q, k, st, o, a))

        diags = []
        for hd, d, cols, g, q, k, st, o, a in units:
            if diag > 8:
                diags.append(None)
                continue
            strips = []
            for blk in range(c // 8):
                rs = slice(8 * blk, 8 * blk + 8)
                gb, qb, kb = g[rs], q[rs], k[rs]
                strip = jnp.zeros((8, c), F32)
                for s in range(8):
                    e = jnp.exp2(gb - gb[s:s + 1])
                    col = jnp.sum(qb * (kb[s:s + 1] * e), axis=-1, keepdims=True)
                    strip = jnp.where(lane == 8 * blk + s, col, strip)
                strips.append(strip)
            diags.append(jnp.where(dm_ref[d] > 0.0, jnp.concatenate(strips, axis=0), 0.0))

        for (hd, d, cols, g, q, k, st, o, a), dg in zip(units, diags):
            v_ref, o_ref = dirs[d][2], dirs[d][4]
            ex, r0 = exps[d], rows[d]
            v = v_ref[pl.ds(r0, c), cols]
            if dg is not None:
                a = a + dg
            o = o + jnp.dot(a.astype(BF16), v, preferred_element_type=F32)
            o_ref[pl.ds(r0, c), cols] = o.astype(BF16)
            kt = (k * jnp.exp2(ex[c:2 * c, cols])).astype(BF16)
            g_all = g[c - 1:c] if d == 0 else g[0:1]
            s_ref[d, hd] = st * jnp.exp2(g_all) + jnp.dot(
                v.astype(F32).T.astype(BF16), kt, preferred_element_type=F32)
        return carry

    lax.fori_loop(0, nch, chunk, 0, unroll=2)


def _gla(rq, kf, kb, ri, lf, lb, diag):
    b, l, w = rq.shape
    c, tc = GLA_CHUNK, GLA_TILE
    n = l // tc
    mst, bm, dm = _gla_constants(c, diag)
    fwd = pl.BlockSpec((None, tc, w), lambda bi, i: (bi, i, 0))
    bwd = pl.BlockSpec((None, tc, w), lambda bi, i: (bi, n - 1 - i, 0))
    out = jax.ShapeDtypeStruct((b, l, w), BF16)
    return pl.pallas_call(
        functools.partial(_gla_body, c=c, nch=tc // c, diag=diag),
        grid=(b, n),
        in_specs=[fwd, fwd, fwd, fwd, bwd, bwd, bwd, bwd,
                  _const_spec(mst.shape), _const_spec(bm.shape), _const_spec(dm.shape)],
        out_specs=[fwd, bwd],
        out_shape=[out, out],
        scratch_shapes=[pltpu.VMEM((2, REC_HEADS, REC_DIM, REC_DIM), F32)],
        compiler_params=_cparams(("parallel", "arbitrary")),
    )(rq, kf, ri, lf, rq, kb, ri, lb, mst, bm, dm)


def _mix_residual(x, ao, o_f, o_b, sg, rw_ref, wo_ref):
    ro = o_f.astype(F32) + o_b.astype(F32)
    parts = []
    for hd in range(REC_HEADS):
        seg = ro[:, hd * REC_DIM:(hd + 1) * REC_DIM]
        ms = jnp.mean(seg * seg, axis=-1, keepdims=True)
        parts.append(seg * lax.rsqrt(ms + NORM_EPS))
    ron = jnp.concatenate(parts, axis=-1) * rw_ref[...] * sg.astype(F32)
    aw = ao.shape[-1]
    mix = (jnp.dot(ao, wo_ref[:aw, :], preferred_element_type=F32)
           + jnp.dot(ron.astype(BF16), wo_ref[aw:, :], preferred_element_type=F32))
    return x + mix


def _mix_ffn_body(*refs, tm, ft, dff, tiles_per_seq):
    rows = [refs[3 * a:3 * a + 3] for a in range(5)]
    (rw_ref, wo_ref, fw_ref, wu_ref, cw_ref, cb_ref, wd_ref, y_ref, acc_ref, u_ref,
     act_ref) = refs[15:]
    x, ao, o_f, o_b, sg = [jnp.concatenate([r[...] for r in trio], axis=0) for trio in rows]
    ext = tm + 2 * HALO
    x1 = _mix_residual(x, ao, o_f, o_b, sg, rw_ref, wo_ref)
    ms = jnp.mean(x1 * x1, axis=-1, keepdims=True)
    h2 = x1 * lax.rsqrt(ms + NORM_EPS) * fw_ref[...]
    pos = lax.rem(pl.program_id(0), tiles_per_seq)
    r = lax.broadcasted_iota(jnp.int32, (ext, 1), 0)
    keep = (((r >= HALO) | (pos != 0)) & ((r < HALO + tm) | (pos != tiles_per_seq - 1)))
    hx = jnp.where(keep, h2, 0.0).astype(BF16)
    acc_ref[...] = x1[HALO:HALO + tm]

    def conv(u, col):
        prev = pltpu.roll(u, 1, axis=0)[HALO:HALO + tm]
        nxt = pltpu.roll(u, ext - 1, axis=0)[HALO:HALO + tm]
        cw = cw_ref[:, pl.ds(col, ft)]
        return (cb_ref[:, pl.ds(col, ft)] + prev * cw[0:1] + u[HALO:HALO + tm] * cw[1:2]
                + nxt * cw[2:3])

    def up_dots(f, slot):
        cg = pl.multiple_of(f * ft, ft)
        cu = pl.multiple_of(dff + f * ft, ft)
        u_ref[slot, 0] = jnp.dot(hx, wu_ref[:, pl.ds(cg, ft)], preferred_element_type=F32)
        u_ref[slot, 1] = jnp.dot(hx, wu_ref[:, pl.ds(cu, ft)], preferred_element_type=F32)

    def finish(f, slot):
        cg = pl.multiple_of(f * ft, ft)
        cu = pl.multiple_of(dff + f * ft, ft)
        gate = conv(u_ref[slot, 0], cg)
        up = conv(u_ref[slot, 1], cu)
        act_ref[:, pl.ds(cg, ft)] = (gate * _sigmoid(gate) * up).astype(BF16)

    nf = dff // ft
    ahead = FFN_LOOKAHEAD
    slots = ahead + 1

    def rotation(i, carry):
        for u in range(slots):
            up_dots(slots * i + u + ahead, (u + ahead) % slots)
            finish(slots * i + u, u)
        return carry

    for f in range(ahead):
        up_dots(f, f)
    looped = (nf - ahead) // slots
    lax.fori_loop(0, looped, rotation, 0)
    for f in range(slots * looped, nf):
        if f + ahead < nf:
            up_dots(f + ahead, (f + ahead) % slots)
        finish(f, f % slots)
    y_ref[...] = acc_ref[...] + jnp.dot(act_ref[...], wd_ref[...], preferred_element_type=F32)


def _mix_ffn(x2, ao, o_f, o_b, sg, rw, w_out, fw, w_up, conv_w, conv_b, w_down, seq_len):
    n, d = x2.shape
    tm, ft = TOKEN_TILE, FFN_FT
    dff = w_down.shape[0]
    hb = tm // HALO
    nhb = n // HALO

    def trio(width):
        return [pl.BlockSpec((HALO, width), lambda i: (jnp.maximum(i * hb - 1, 0), 0)),
                pl.BlockSpec((tm, width), lambda i: (i, 0)),
                pl.BlockSpec((HALO, width), lambda i: (jnp.minimum((i + 1) * hb, nhb - 1), 0))]

    body = functools.partial(_mix_ffn_body, tm=tm, ft=ft, dff=dff, tiles_per_seq=seq_len // tm)
    row_inputs = (x2, ao, o_f, o_b, sg)
    consts = (rw, w_out, fw, w_up, conv_w, conv_b, w_down)
    return pl.pallas_call(
        body,
        grid=(n // tm,),
        in_specs=[spec for a in row_inputs for spec in trio(a.shape[1])]
                 + [_const_spec(a.shape) for a in consts],
        out_specs=pl.BlockSpec((tm, d), lambda i: (i, 0)),
        out_shape=jax.ShapeDtypeStruct((n, d), F32),
        scratch_shapes=[pltpu.VMEM((tm, d), F32),
                        pltpu.VMEM((FFN_LOOKAHEAD + 1, 2, tm + 2 * HALO, ft), F32),
                        pltpu.VMEM((tm, dff), BF16)],
        compiler_params=_cparams(("parallel",)),
    )(*[a for a in row_inputs for _ in range(3)], *consts)


def _bf16_split3(x):
    x = x.astype(F32)
    a = x.astype(BF16)
    r = x - a.astype(F32)
    b = r.astype(BF16)
    c = (r - b.astype(F32)).astype(BF16)
    return a, b, c


def _alibi_operands():
    start = 2.0 ** (-8.0 / ATTN_HEADS)
    slopes = np.array([start ** (i + 1) for i in range(ATTN_HEADS)], np.float32)
    cs = jnp.asarray(slopes * np.float32(LOG2E), F32)
    groups = GROUP_W // ATTN_QKDIM
    nrows = max(ATTN_BK, TOKEN_TILE)
    pos = np.arange(nrows) % ATTN_BK
    kaug = np.zeros((nrows, groups, 128), np.float32)
    kaug[:, :, ATTN_QKDIM:ATTN_QKDIM + 3] = (pos - pos % 16)[:, None, None]
    kaug[:, :, ATTN_QKDIM + 3:ATTN_QKDIM + 6] = (pos % 16)[:, None, None]
    c1, c2, c3 = _bf16_split3(cs)
    six = jnp.stack([c1, c2, c3, c1, c2, c3], axis=1).astype(F32)
    rows = jnp.zeros((ATTN_HEADS, ATTN_QKDIM), F32).at[:, :6].set(six)
    aug = jnp.broadcast_to(rows[..., None], rows.shape + (ATTN_BQ,))
    return cs, jnp.asarray(kaug.reshape(nrows, groups * 128), F32), aug


def _trunk(x, norm_mix_w, w_in, q_norm_w, k_norm_w, lam, attn_out_norm_w, lb_fwd, lb_bwd,
           rec_out_norm_w, w_out, norm_ffn_w, w_up, conv_w, conv_b, w_down):
    b, l, d = x.shape
    n = b * l
    x2 = x.reshape(n, d)

    seg = np.kron(np.eye(GROUP_W // ATTN_QKDIM, dtype=np.float32),
                  np.full((ATTN_QKDIM, ATTN_QKDIM), 1.0 / ATTN_QKDIM, np.float32))
    reps = GROUP_W // ATTN_QKDIM
    lbf = jnp.cumsum(jax.nn.softmax(lb_fwd.astype(F32), axis=0), axis=0)[0][None]
    lbb = jnp.cumsum(jax.nn.softmax(lb_bwd.astype(F32), axis=0), axis=0)[0][None]
    cs, kaug, aug = _alibi_operands()
    qt, kp, vt, rq, kf, kb, lf, lb, ri, sg = _inproj(
        x2, norm_mix_w[None], w_in.astype(BF16), jnp.asarray(seg, BF16),
        jnp.tile(q_norm_w, reps)[None], jnp.tile(k_norm_w, reps)[None], lbf, lbb, kaug, b)

    wcol = jnp.broadcast_to(attn_out_norm_w.astype(F32)[:, None], (ATTN_VDIM, ATTN_BQ))
    score_bound = (1.02 * ATTN_QKDIM ** 0.5 * LOG2E) * (jnp.max(jnp.abs(q_norm_w))
                                                         * jnp.max(jnp.abs(k_norm_w)))
    attn_args = (cs, lam.reshape(1), qt, kp.reshape(b, l, kp.shape[-1]), vt, aug, wcol)
    ao = lax.cond(2.0 * score_bound <= ATTN_FIXED_SHIFT_RANGE,
                  functools.partial(_attn, online=False),
                  functools.partial(_attn, online=True), *attn_args)

    shape3 = (b, l, GROUP_W)
    decay_log2 = -GLA_FACTORED_DIAG * jnp.log2(jnp.minimum(jnp.min(lbf), jnp.min(lbb)))
    gla_args = [t.reshape(shape3) for t in (rq, kf, kb, ri, lf, lb)]
    o_f, o_b = lax.cond(decay_log2 <= GLA_FACTORED_RANGE,
                        functools.partial(_gla, diag=GLA_FACTORED_DIAG),
                        functools.partial(_gla, diag=8), *gla_args)

    y = _mix_ffn(x2, ao.reshape(n, GROUP_W), o_f.reshape(n, GROUP_W), o_b.reshape(n, GROUP_W),
                 sg, jnp.tile(rec_out_norm_w, REC_HEADS)[None], w_out.astype(BF16),
                 norm_ffn_w[None], w_up.astype(BF16), conv_w, conv_b[None], w_down.astype(BF16), l)
    return y.reshape(b, l, d)


def kernel(x_prompt, x_sample, norm_mix_w, w_in, q_norm_w, k_norm_w, lambda_q1, lambda_k1,
           lambda_q2, lambda_k2, attn_out_norm_w, lb_fwd, lb_bwd, rec_out_norm_w, w_out,
           norm_ffn_w, w_up, conv_w, conv_b, w_down):
    assert norm_mix_w.shape[0] == 1, "single-layer trunk"
    lam = (jnp.exp(jnp.sum(lambda_q1[0].astype(F32) * lambda_k1[0].astype(F32)))
           - jnp.exp(jnp.sum(lambda_q2[0].astype(F32) * lambda_k2[0].astype(F32))) + LAM_INIT)
    params = (norm_mix_w[0], w_in[0], q_norm_w[0], k_norm_w[0], lam, attn_out_norm_w[0],
              lb_fwd, lb_bwd, rec_out_norm_w[0], w_out[0], norm_ffn_w[0], w_up[0], conv_w[0],
              conv_b[0], w_down[0])
    return (_trunk(x_prompt, *params), _trunk(x_sample, *params))
```
